```python
import math
import jax
import jax.numpy as jnp
from jax import lax
import numpy as np

D_MODEL = 1024
BATCH = 8
SEQ = 2048
DEPTH = 4
DEC_BATCH = 128
DEC_SEQ = 8
PAST_LEN = 16384
PAGE_SIZE = 128

EPS = 1e-6
CONV_W = 4

LRU_WIDTH = D_MODEL
LRU_BLOCKS = 16
LRU_BLOCK = LRU_WIDTH // LRU_BLOCKS
LRU_C = 8.0

GLA_HEADS = 4
GLA_DK = D_MODEL // (2 * GLA_HEADS)
GLA_DV = D_MODEL // GLA_HEADS
GLA_RANK = 16
GLA_TAU = 16.0
GLA_CHUNK = 64

SSD_HEADS = 16
SSD_HEADDIM = D_MODEL // SSD_HEADS
SSD_WIDTH = SSD_HEADS * SSD_HEADDIM
SSD_GROUPS = 2
SSD_STATE = 64
SSD_CHUNK = 64
SSD_CONV_CH = SSD_WIDTH + 2 * SSD_GROUPS * SSD_STATE

MEM_TOKENS = 256
XA_HEADS = 4
XA_HD = D_MODEL // XA_HEADS

D_FF = 256 * (-(-(8 * D_MODEL) // (3 * 256)))

N_BRANCH = 3
IN_SIZES = (LRU_WIDTH, LRU_WIDTH,
            GLA_HEADS * GLA_DK, GLA_HEADS * GLA_DK, GLA_HEADS * GLA_DV, GLA_HEADS * GLA_DV, GLA_RANK,
            SSD_WIDTH, SSD_CONV_CH, SSD_HEADS,
            N_BRANCH * D_MODEL)
N_IN = sum(IN_SIZES)

kernel_name = 'hybrid_lru_gla_ssd_decoder_step'


def _split_points(sizes):
    pts, acc = [], 0
    for s in sizes[:-1]:
        acc += s
        pts.append(acc)
    return pts


def rmsnorm(x, g):
    xf = x.astype(jnp.float32)
    xf = xf * lax.rsqrt(jnp.mean(xf * xf, axis=-1, keepdims=True) + EPS)
    return (xf * g.astype(jnp.float32)).astype(x.dtype)


def grouped_rmsnorm(y, g, groups):
    b_, l_, w_ = y.shape
    yf = y.astype(jnp.float32).reshape(b_, l_, groups, w_ // groups)
    yf = yf * lax.rsqrt(jnp.mean(yf * yf, axis=-1, keepdims=True) + EPS)
    return (yf.reshape(b_, l_, w_) * g.astype(jnp.float32)).astype(y.dtype)


def causal_conv(x, buf, w, b):
    l_ = x.shape[1]
    xp = jnp.concatenate([buf.astype(x.dtype), x], axis=1)
    y = b
    for k in range(CONV_W):
        y = y + w[k] * xp[:, k:k + l_]
    return y, xp[:, xp.shape[1] - (CONV_W - 1):]


def _to_chunks(t, c):
    b_, l_ = t.shape[:2]
    return jnp.moveaxis(t.reshape((b_, l_ // c, c) + t.shape[2:]), 1, 0)


def _from_chunks(t):
    nc, b_, c = t.shape[:3]
    return jnp.moveaxis(t, 0, 1).reshape((b_, nc * c) + t.shape[3:])


def _lru_combine(left, right):
    a1, b1 = left
    a2, b2 = right
    return a1 * a2, a2 * b1 + b2


def rg_lru_branch(x_in, gate_in, h0, conv_buf, conv_w, conv_b, wa, ba, wx, bx, lam):
    b_, l_, _ = x_in.shape
    xc, new_buf = causal_conv(x_in, conv_buf, conv_w, conv_b)
    xb = xc.reshape(b_, l_, LRU_BLOCKS, LRU_BLOCK)
    r = jax.nn.sigmoid(jnp.einsum('blki,kij->blkj', xb, wa).reshape(b_, l_, LRU_WIDTH) + ba)
    i = jax.nn.sigmoid(jnp.einsum('blki,kij->blkj', xb, wx).reshape(b_, l_, LRU_WIDTH) + bx)
    log_a = -LRU_C * r.astype(jnp.float32) * jax.nn.softplus(-lam.astype(jnp.float32))
    a = jnp.exp(log_a)
    u = jnp.sqrt(-jnp.expm1(2.0 * log_a)) * (i * xc).astype(jnp.float32)
    a_cum, h_from_zero = lax.associative_scan(_lru_combine, (a, u), axis=1)
    h = a_cum * h0.astype(jnp.float32)[:, None] + h_from_zero
    y = h.astype(x_in.dtype) * jax.nn.gelu(gate_in)
    return y, h[:, -1], new_buf


def gla_chunked(q, k, v, g, s0):
    l_ = q.shape[1]
    c = math.gcd(l_, GLA_CHUNK)
    mask = jnp.tril(jnp.ones((c, c), dtype=bool))
    f32 = jnp.float32
    xs = tuple(_to_chunks(t.astype(f32), c) for t in (q * GLA_DK ** -0.5, k, v, g))

    def step(s, inp):
        qi, ki, vi, gi = inp
        b = jnp.cumsum(gi, axis=1)
        b_last = b[:, -1]
        qt = qi * jnp.exp(b)
        kt = ki * jnp.exp(-b)
        ke = ki * jnp.exp(b_last[:, None] - b)
        att = jnp.where(mask, jnp.einsum('bihd,bjhd->bhij', qt, kt), 0.0)
        o = jnp.einsum('bhij,bjhv->bihv', att, vi) + jnp.einsum('bihd,bhdv->bihv', qt, s)
        s = jnp.exp(b_last)[..., None] * s + jnp.einsum('bjhd,bjhv->bhdv', ke, vi)
        return s, o

    s, o = lax.scan(step, s0.astype(f32), xs)
    return _from_chunks(o), s


def ssd_chunked(x, dt, a, bm, cm, h0):
    b_, l_ = x.shape[:2]
    hpg = SSD_HEADS // SSD_GROUPS
    c = math.gcd(l_, SSD_CHUNK)
    f32 = jnp.float32
    mask = jnp.tril(jnp.ones((c, c), dtype=bool))[None, :, :, None, None]
    xg = x.astype(f32).reshape(b_, l_, SSD_GROUPS, hpg, SSD_HEADDIM)
    dtg = dt.astype(f32).reshape(b_, l_, SSD_GROUPS, hpg)
    dag = dtg * a.astype(f32).reshape(SSD_GROUPS, hpg)
    xs = tuple(_to_chunks(t, c) for t in (xg, dtg, dag, bm.astype(f32), cm.astype(f32)))
    hg0 = h0.astype(f32).reshape(b_, SSD_GROUPS, hpg, SSD_HEADDIM, SSD_STATE)

    def step(h, inp):
        xi, dti, dai, bi, ci = inp
        cum = jnp.cumsum(dai, axis=1)
        seg = cum[:, :, None] - cum[:, None, :]
        decay = jnp.exp(jnp.where(mask, seg, -jnp.inf))
        cb = jnp.einsum('bign,bjgn->bijg', ci, bi)
        m = cb[..., None] * decay * dti[:, None]
        y = jnp.einsum('bijgh,bjghp->bighp', m, xi)
        y = y + jnp.einsum('bign,bghpn->bighp', ci, h) * jnp.exp(cum)[..., None]
        last = cum[:, -1]
        w = jnp.exp(last[:, None] - cum) * dti
        h = jnp.exp(last)[..., None, None] * h + jnp.einsum('bjgh,bjghp,bjgn->bghpn', w, xi, bi)
        return h, y

    h, y = lax.scan(step, hg0, xs)
    y = _from_chunks(y).reshape(b_, l_, SSD_HEADS, SSD_HEADDIM)
    return y, h.reshape(b_, SSD_HEADS, SSD_HEADDIM, SSD_STATE)


def memory_kv(mem, g, wk, wv):
    b_, m_, _ = mem.shape
    mn = rmsnorm(mem, g)
    return ((mn @ wk).reshape(b_, m_, XA_HEADS, XA_HD), (mn @ wv).reshape(b_, m_, XA_HEADS, XA_HD))


def cross_attention(h, k, v, wq, wo):
    b_, l_, _ = h.shape
    q = (h @ wq).reshape(b_, l_, XA_HEADS, XA_HD)
    s = jnp.einsum('blhd,bmhd->bhlm', q, k.astype(q.dtype)).astype(jnp.float32) * XA_HD ** -0.5
    p = jax.nn.softmax(s, axis=-1).astype(h.dtype)
    o = jnp.einsum('bhlm,bmhd->blhd', p, v.astype(h.dtype)).reshape(b_, l_, D_MODEL)
    return o @ wo


def hybrid_layer(x, mem_k, mem_v, lru_h, lru_conv, gla_s, ssd_h, ssd_conv, W):
    b_, l_, _ = x.shape
    h = rmsnorm(x, W['norm_mix'])
    u = h @ W['w_in']
    (lru_x, lru_gate, gq, gk, gv, gr, g_low, sz, sxbc, sdt, gate_logits) = jnp.split(
        u, _split_points(IN_SIZES), axis=-1)

    y_lru, lru_h_new, lru_conv_new = rg_lru_branch(
        lru_x, lru_gate, lru_h, lru_conv, W['lru_conv_w'], W['lru_conv_b'],
        W['lru_wa'], W['lru_ba'], W['lru_wx'], W['lru_bx'], W['lru_lambda'])
    b_lru = y_lru @ W['w_lru_out']

    g_log = jax.nn.log_sigmoid((g_low @ W['gla_w_alpha'] + W['gla_b_alpha']).astype(jnp.float32)) / GLA_TAU
    o, gla_s_new = gla_chunked(
        gq.reshape(b_, l_, GLA_HEADS, GLA_DK), gk.reshape(b_, l_, GLA_HEADS, GLA_DK),
        gv.reshape(b_, l_, GLA_HEADS, GLA_DV), g_log.reshape(b_, l_, GLA_HEADS, GLA_DK), gla_s)
    o = rmsnorm(o, W['gla_norm']).reshape(b_, l_, GLA_HEADS * GLA_DV).astype(x.dtype)
    b_gla = (o * jax.nn.silu(gr)) @ W['w_gla_out']

    xbc, ssd_conv_new = causal_conv(sxbc, ssd_conv, W['ssd_conv_w'], W['ssd_conv_b'])
    xbc = jax.nn.silu(xbc)
    sx, sb, sc = jnp.split(xbc, [SSD_WIDTH, SSD_WIDTH + SSD_GROUPS * SSD_STATE], axis=-1)
    sx = sx.reshape(b_, l_, SSD_HEADS, SSD_HEADDIM)
    dt = jax.nn.softplus((sdt + W['ssd_dt_bias']).astype(jnp.float32))
    a = -jnp.exp(W['ssd_a_log'].astype(jnp.float32))
    y, ssd_h_new = ssd_chunked(sx, dt, a, sb.reshape(b_, l_, SSD_GROUPS, SSD_STATE),
                               sc.reshape(b_, l_, SSD_GROUPS, SSD_STATE), ssd_h)
    y = (y + W['ssd_d'].astype(jnp.float32)[:, None] * sx.astype(jnp.float32)).astype(x.dtype)
    y = grouped_rmsnorm(y.reshape(b_, l_, SSD_WIDTH) * jax.nn.silu(sz), W['ssd_norm'], SSD_GROUPS)
    b_ssd = y @ W['w_ssd_out']

    gates = jax.nn.sigmoid(gate_logits).reshape(b_, l_, N_BRANCH, D_MODEL)
    merged = gates[:, :, 0] * b_lru + gates[:, :, 1] * b_gla + gates[:, :, 2] * b_ssd
    x = x + merged @ W['w_mix_out']

    x = x + cross_attention(rmsnorm(x, W['norm_xattn']), mem_k, mem_v, W['w_xq'], W['w_xo'])

    h = rmsnorm(x, W['norm_ffn'])
    x = x + (jax.nn.silu(h @ W['w_ffn_gate']) * (h @ W['w_ffn_up'])) @ W['w_ffn_down']
    return x, (lru_h_new, lru_conv_new, gla_s_new, ssd_h_new, ssd_conv_new)


def setup_inputs(seed: int = 0) -> dict:
    key = jax.random.key(seed)
    keys = jax.random.split(key, 64)
    counter = [0]
    f32 = jnp.float32

    def nk():
        k = keys[counter[0]]
        counter[0] += 1
        return k

    def nrm(shape, scale=1.0):
        return scale * jax.random.normal(nk(), shape, f32)

    def gain(shape):
        return 1.0 + nrm(shape, 0.01)

    lam_u = jax.random.uniform(nk(), (DEPTH, LRU_WIDTH), f32, minval=0.9, maxval=0.999)
    dt0 = jnp.exp(jax.random.uniform(nk(), (DEPTH, SSD_HEADS), f32,
                                     minval=math.log(1e-3), maxval=math.log(1e-1)))
    a0 = jax.random.uniform(nk(), (DEPTH, SSD_HEADS), f32, minval=1.0, maxval=16.0)
    return {
        'x_prompt': nrm((BATCH, SEQ, D_MODEL)),
        'x_sample': nrm((DEC_BATCH, DEC_SEQ, D_MODEL)),
        'mem_prompt': nrm((BATCH, MEM_TOKENS, D_MODEL)),
        'state_lru_h': nrm((DEPTH, DEC_BATCH, LRU_WIDTH), 0.5),
        'state_lru_conv': nrm((DEPTH, DEC_BATCH, CONV_W - 1, LRU_WIDTH)),
        'state_gla_S': nrm((DEPTH, DEC_BATCH, GLA_HEADS, GLA_DK, GLA_DV), 0.3),
        'state_ssd_h': nrm((DEPTH, DEC_BATCH, SSD_HEADS, SSD_HEADDIM, SSD_STATE), 0.1),
        'state_ssd_conv': nrm((DEPTH, DEC_BATCH, CONV_W - 1, SSD_CONV_CH)),
        'cache_mem_k': nrm((DEPTH, DEC_BATCH, MEM_TOKENS, XA_HEADS, XA_HD)),
        'cache_mem_v': nrm((DEPTH, DEC_BATCH, MEM_TOKENS, XA_HEADS, XA_HD)),
        'norm_mix': gain((DEPTH, D_MODEL)),
        'w_in': nrm((DEPTH, D_MODEL, N_IN), D_MODEL ** -0.5),
        'lru_conv_w': nrm((DEPTH, CONV_W, LRU_WIDTH), CONV_W ** -0.5),
        'lru_conv_b': nrm((DEPTH, LRU_WIDTH), 0.01),
        'lru_wa': nrm((DEPTH, LRU_BLOCKS, LRU_BLOCK, LRU_BLOCK), LRU_BLOCK ** -0.5),
        'lru_ba': nrm((DEPTH, LRU_WIDTH), 0.01),
        'lru_wx': nrm((DEPTH, LRU_BLOCKS, LRU_BLOCK, LRU_BLOCK), LRU_BLOCK ** -0.5),
        'lru_bx': nrm((DEPTH, LRU_WIDTH), 0.01),
        'lru_lambda': jnp.log(lam_u) - jnp.log1p(-lam_u),
        'w_lru_out': nrm((DEPTH, LRU_WIDTH, D_MODEL), LRU_WIDTH ** -0.5),
        'gla_w_alpha': nrm((DEPTH, GLA_RANK, GLA_HEADS * GLA_DK), GLA_RANK ** -0.5),
        'gla_b_alpha': nrm((DEPTH, GLA_HEADS * GLA_DK), 0.1),
        'gla_norm': gain((DEPTH, GLA_DV)),
        'w_gla_out': nrm((DEPTH, GLA_HEADS * GLA_DV, D_MODEL), (GLA_HEADS * GLA_DV) ** -0.5),
        'ssd_conv_w': nrm((DEPTH, CONV_W, SSD_CONV_CH), CONV_W ** -0.5),
        'ssd_conv_b': nrm((DEPTH, SSD_CONV_CH), 0.01),
        'ssd_dt_bias': dt0 + jnp.log(-jnp.expm1(-dt0)),
        'ssd_a_log': jnp.log(a0),
        'ssd_d': 1.0 + nrm((DEPTH, SSD_HEADS), 0.1),
        'ssd_norm': gain((DEPTH, SSD_WIDTH)),
        'w_ssd_out': nrm((DEPTH, SSD_WIDTH, D_MODEL), SSD_WIDTH ** -0.5),
        'w_mix_out': nrm((DEPTH, D_MODEL, D_MODEL), D_MODEL ** -0.5),
        'norm_xattn': gain((DEPTH, D_MODEL)),
        'norm_mem': gain((DEPTH, D_MODEL)),
        'w_xq': nrm((DEPTH, D_MODEL, D_MODEL), D_MODEL ** -0.5),
        'w_xk': nrm((DEPTH, D_MODEL, D_MODEL), D_MODEL ** -0.5),
        'w_xv': nrm((DEPTH, D_MODEL, D_MODEL), D_MODEL ** -0.5),
        'w_xo': nrm((DEPTH, D_MODEL, D_MODEL), D_MODEL ** -0.5),
        'norm_ffn': gain((DEPTH, D_MODEL)),
        'w_ffn_gate': nrm((DEPTH, D_MODEL, D_FF), D_MODEL ** -0.5),
        'w_ffn_up': nrm((DEPTH, D_MODEL, D_FF), D_MODEL ** -0.5),
        'w_ffn_down': nrm((DEPTH, D_FF, D_MODEL), D_FF ** -0.5),
        'norm_final': gain((D_MODEL,)),
    }


def reference(x_prompt, x_sample, mem_prompt, state_lru_h, state_lru_conv, state_gla_S, state_ssd_h,
              state_ssd_conv, cache_mem_k, cache_mem_v, norm_mix, w_in, lru_conv_w, lru_conv_b, lru_wa,
              lru_ba, lru_wx, lru_bx, lru_lambda, w_lru_out, gla_w_alpha, gla_b_alpha, gla_norm, w_gla_out,
              ssd_conv_w, ssd_conv_b, ssd_dt_bias, ssd_a_log, ssd_d, ssd_norm, w_ssd_out, w_mix_out,
              norm_xattn, norm_mem, w_xq, w_xk, w_xv, w_xo, norm_ffn, w_ffn_gate, w_ffn_up, w_ffn_down,
              norm_final):
    stacked = {
        'norm_mix': norm_mix, 'w_in': w_in,
        'lru_conv_w': lru_conv_w, 'lru_conv_b': lru_conv_b, 'lru_wa': lru_wa, 'lru_ba': lru_ba,
        'lru_wx': lru_wx, 'lru_bx': lru_bx, 'lru_lambda': lru_lambda, 'w_lru_out': w_lru_out,
        'gla_w_alpha': gla_w_alpha, 'gla_b_alpha': gla_b_alpha, 'gla_norm': gla_norm, 'w_gla_out': w_gla_out,
        'ssd_conv_w': ssd_conv_w, 'ssd_conv_b': ssd_conv_b, 'ssd_dt_bias': ssd_dt_bias,
        'ssd_a_log': ssd_a_log, 'ssd_d': ssd_d, 'ssd_norm': ssd_norm, 'w_ssd_out': w_ssd_out,
        'w_mix_out': w_mix_out, 'norm_xattn': norm_xattn, 'norm_mem': norm_mem,
        'w_xq': w_xq, 'w_xk': w_xk, 'w_xv': w_xv, 'w_xo': w_xo,
        'norm_ffn': norm_ffn, 'w_ffn_gate': w_ffn_gate, 'w_ffn_up': w_ffn_up, 'w_ffn_down': w_ffn_down,
    }
    f32 = jnp.float32
    bp = x_prompt.shape[0]
    xp, xs = x_prompt, x_sample
    p_out = [[] for _ in range(7)]
    s_out = [[] for _ in range(5)]
    for l in range(DEPTH):
        W = {name: arr[l] for name, arr in stacked.items()}
        mk, mv = memory_kv(mem_prompt, W['norm_mem'], W['w_xk'], W['w_xv'])
        xp, p_st = hybrid_layer(
            xp, mk, mv,
            jnp.zeros((bp, LRU_WIDTH), f32),
            jnp.zeros((bp, CONV_W - 1, LRU_WIDTH), xp.dtype),
            jnp.zeros((bp, GLA_HEADS, GLA_DK, GLA_DV), f32),
            jnp.zeros((bp, SSD_HEADS, SSD_HEADDIM, SSD_STATE), f32),
            jnp.zeros((bp, CONV_W - 1, SSD_CONV_CH), xp.dtype),
            W)
        xs, s_st = hybrid_layer(
            xs, cache_mem_k[l], cache_mem_v[l], state_lru_h[l], state_lru_conv[l],
            state_gla_S[l], state_ssd_h[l], state_ssd_conv[l], W)
        for lst, val in zip(p_out, p_st + (mk, mv)):
            lst.append(val)
        for lst, val in zip(s_out, s_st):
            lst.append(val)
    y_prompt = rmsnorm(xp, norm_final)
    y_sample = rmsnorm(xs, norm_final)
    p_lru_h, p_lru_conv, p_gla_S, p_ssd_h, p_ssd_conv, p_mem_k, p_mem_v = [jnp.stack(t, axis=0) for t in p_out]
    s_lru_h, s_lru_conv, s_gla_S, s_ssd_h, s_ssd_conv = [jnp.stack(t, axis=0) for t in s_out]
    return (y_prompt, y_sample, p_lru_h, p_lru_conv, p_gla_S, p_ssd_h, p_ssd_conv, p_mem_k, p_mem_v,
            s_lru_h, s_lru_conv, s_gla_S, s_ssd_h, s_ssd_conv)
```

```python
import functools

import jax
import jax.numpy as jnp
from jax import lax
from jax.experimental import pallas as pl
from jax.experimental.pallas import tpu as pltpu

f32 = jnp.float32
bf16 = jnp.bfloat16

D = 1024
DEPTH = 4
EPS = 1e-6
CONV_W = 4
LRU_C = 8.0
LRU_TILE = 256
GLA_HEADS, GLA_DK, GLA_DV, GLA_RANK, GLA_TAU, GLA_CHUNK = 4, 128, 256, 16, 16.0, 64
SSD_HEADS, SSD_P, SSD_GROUPS, SSD_N, SSD_CHUNK = 16, 64, 2, 64, 128
SSD_CH = D + 2 * SSD_GROUPS * SSD_N
XA_HEADS, XA_HD, MEM = 4, 256, 256
D_FF = 2816
LANE = 128
SUB = 8
W_LRU, W_GLA, W_SSD, W_GATE = 2 * D, 3 * D + LANE, D + SSD_CH + LANE, 3 * D
VMEM_LIMIT = 56 * 1024 * 1024


def _cp(n_axes):
    return pltpu.CompilerParams(dimension_semantics=("arbitrary",) * n_axes,
                                vmem_limit_bytes=VMEM_LIMIT)


def _const_spec(shape):
    nd = len(shape)
    return pl.BlockSpec(shape, lambda *_: (0,) * nd, pipeline_mode=pl.Buffered(1))


def _rms(x, g):
    return x * lax.rsqrt(jnp.mean(x * x, axis=-1, keepdims=True) + EPS) * g


def _dot(a, b):
    return jnp.dot(a.astype(bf16), b.astype(bf16), preferred_element_type=f32)


def _dot_nt(a, b):
    return lax.dot_general(a.astype(bf16), b.astype(bf16), (((1,), (1,)), ((), ())),
                           preferred_element_type=f32)


def _split(x):
    hi = x.astype(bf16)
    return hi, (x - hi.astype(f32)).astype(bf16)


def _dot_split(m, x):
    hi, lo = _split(x)
    return (jnp.dot(m, hi, preferred_element_type=f32) + jnp.dot(m, lo, preferred_element_type=f32))


def _bdot(a, b, nb, rows, mode):
    a = a.astype(bf16)
    b = b.astype(bf16)
    if mode == "tn":
        if nb == 1:
            return lax.dot_general(a, b, (((0,), (0,)), ((), ())), preferred_element_type=f32)[None]
        a3 = a.reshape(nb, rows, a.shape[-1])
        b3 = b.reshape(nb, rows, b.shape[-1])
        return jnp.einsum("bki,bkj->bij", a3, b3, preferred_element_type=f32)
    if nb == 1:
        if mode == "nn":
            return jnp.dot(a, b[0], preferred_element_type=f32)
        return lax.dot_general(a, b[0], (((1,), (1,)), ((), ())), preferred_element_type=f32)
    a3 = a.reshape(nb, rows, a.shape[-1])
    eq = "bik,bkj->bij" if mode == "nn" else "bik,bjk->bij"
    out = jnp.einsum(eq, a3, b, preferred_element_type=f32)
    return out.reshape(nb * rows, out.shape[-1])


def _seq_masks(r, c):
    ri = lax.broadcasted_iota(jnp.int32, (r, r), 0)
    ci = lax.broadcasted_iota(jnp.int32, (r, r), 1)
    if r == c:
        same = jnp.full((r, r), True)
    else:
        shift = c.bit_length() - 1
        same = (ri >> shift) == (ci >> shift)
    causal = same & (ri >= ci)
    return causal, same


def _conv_carry(x, tail, w_ref, b_ref):
    xe = jnp.concatenate([tail, x], axis=0)
    y = b_ref[...] + w_ref[0:1, :] * pltpu.roll(xe, 3, 0)[SUB:]
    y = y + w_ref[1:2, :] * pltpu.roll(xe, 2, 0)[SUB:]
    y = y + w_ref[2:3, :] * pltpu.roll(xe, 1, 0)[SUB:]
    return y + w_ref[3:4, :] * x


def _conv_groups(x, bufpad, w_ref, b_ref):
    rows = x.shape[0]
    l = lax.broadcasted_iota(jnp.int32, (rows, 1), 0) & (SUB - 1)
    y = b_ref[...]
    for k in range(CONV_W - 1):
        s = CONV_W - 1 - k
        sh = jnp.where(l >= s, pltpu.roll(x, s, 0), pltpu.roll(bufpad, rows - SUB + s, 0))
        y = y + w_ref[k:k + 1, :] * sh
    return y + w_ref[3:4, :] * x


def _norm_matmul_kernel(x_ref, g_ref, w_ref, o_ref):
    o_ref[...] = _dot(_rms(x_ref[...], g_ref[...]), w_ref[...])


def _norm_matmul(x, g, w, tm):
    m, k = x.shape
    n = w.shape[1]
    return pl.pallas_call(
        _norm_matmul_kernel,
        out_shape=jax.ShapeDtypeStruct((m, n), f32),
        grid=(m // tm,),
        in_specs=[pl.BlockSpec((tm, k), lambda i: (i, 0)), _const_spec((1, k)), _const_spec((k, n))],
        out_specs=pl.BlockSpec((tm, n), lambda i: (i, 0)),
        compiler_params=_cp(1),
    )(x, g, w)


def _final_norm_kernel(x_ref, g_ref, o_ref):
    o_ref[...] = _rms(x_ref[...], g_ref[...])


def _final_norm(x, g, tm):
    m, k = x.shape
    return pl.pallas_call(
        _final_norm_kernel,
        out_shape=jax.ShapeDtypeStruct((m, k), f32),
        grid=(m // tm,),
        in_specs=[pl.BlockSpec((tm, k), lambda i: (i, 0)), _const_spec((1, k))],
        out_specs=pl.BlockSpec((tm, k), lambda i: (i, 0)),
        compiler_params=_cp(1),
    )(x, g)


def _lru_math(u, xc, lam_ref, wa_ref, ba_ref, wx_ref, bx_ref):
    xcb = xc.astype(bf16)
    nt = D // LRU_TILE
    r_pre = jnp.concatenate([jnp.dot(xcb[:, i * LRU_TILE:(i + 1) * LRU_TILE], wa_ref[i],
                                     preferred_element_type=f32) for i in range(nt)], axis=1)
    i_pre = jnp.concatenate([jnp.dot(xcb[:, i * LRU_TILE:(i + 1) * LRU_TILE], wx_ref[i],
                                     preferred_element_type=f32) for i in range(nt)], axis=1)
    r = jax.nn.sigmoid(r_pre + ba_ref[...])
    ig = jax.nn.sigmoid(i_pre + bx_ref[...])
    log_a = -LRU_C * r * jax.nn.softplus(-lam_ref[...])
    a = jnp.exp(log_a)
    t = jnp.tanh(log_a)
    mult = jnp.sqrt(-2.0 * t / (1.0 - t))
    return a, mult * (ig * xc)


def _group_scan(a, b):
    l = lax.broadcasted_iota(jnp.int32, (a.shape[0], 1), 0) & (SUB - 1)
    for s in (1, 2, 4):
        m = l >= s
        b = jnp.where(m, a * pltpu.roll(b, s, 0) + b, b)
        a = jnp.where(m, a * pltpu.roll(a, s, 0), a)
    return a, b


def _lru_prompt_kernel(x_ref, g_ref, w_ref, cw_ref, cb_ref, lam_ref, wa_ref, ba_ref, wx_ref, bx_ref,
                       y_ref, h_ref, tail_ref, hc_scr, tail_scr):
    t = pl.program_id(1)

    @pl.when(t == 0)
    def _():
        hc_scr[...] = jnp.zeros_like(hc_scr)
        tail_scr[...] = jnp.zeros_like(tail_scr)

    u = _dot(_rms(x_ref[...], g_ref[...]), w_ref[...])
    x = u[:, :D]
    gate = u[:, D:]
    xc = _conv_carry(x, tail_scr[...], cw_ref, cb_ref)
    a, b = _lru_math(u, xc, lam_ref, wa_ref, ba_ref, wx_ref, bx_ref)
    a, b = _group_scan(a, b)
    carry = hc_scr[0:1, :]
    hs = []
    for j in range(x.shape[0] // SUB):
        hj = a[j * SUB:(j + 1) * SUB] * carry + b[j * SUB:(j + 1) * SUB]
        carry = hj[SUB - 1:SUB]
        hs.append(hj)
    h = jnp.concatenate(hs, axis=0)
    y_ref[...] = h * jax.nn.gelu(gate)
    hc = jnp.broadcast_to(carry, (SUB, D))
    hc_scr[...] = hc
    tail_scr[...] = x[x.shape[0] - SUB:]
    h_ref[0] = hc
    tail_ref[0] = x[x.shape[0] - SUB:]


def _lru_sample_kernel(u_ref, h0_ref, bp_ref, cw_ref, cb_ref, lam_ref, wa_ref, ba_ref, wx_ref, bx_ref,
                       y_ref, h_ref):
    u = u_ref[...]
    x = u[:, :D]
    gate = u[:, D:]
    xc = _conv_groups(x, bp_ref[...], cw_ref, cb_ref)
    a, b = _lru_math(u, xc, lam_ref, wa_ref, ba_ref, wx_ref, bx_ref)
    a, b = _group_scan(a, b)
    h = a * h0_ref[...] + b
    h_ref[...] = h
    y_ref[...] = h * jax.nn.gelu(gate)


def _lru_param_specs():
    nt = D // LRU_TILE
    return [_const_spec((CONV_W, D)), _const_spec((1, D)), _const_spec((1, D)),
            _const_spec((nt, LRU_TILE, LRU_TILE)), _const_spec((1, D)),
            _const_spec((nt, LRU_TILE, LRU_TILE)), _const_spec((1, D))]


def _lru_prompt(x, g, w, params, nseq, seqlen, tt):
    nt = seqlen // tt
    row = lambda b, t: (b * nt + t, 0)
    seq = lambda b, t: (b, 0, 0)
    return pl.pallas_call(
        _lru_prompt_kernel,
        out_shape=(jax.ShapeDtypeStruct((nseq * seqlen, D), f32),
                   jax.ShapeDtypeStruct((nseq, SUB, D), f32),
                   jax.ShapeDtypeStruct((nseq, SUB, D), f32)),
        grid=(nseq, nt),
        in_specs=[pl.BlockSpec((tt, D), row), _const_spec((1, D)), _const_spec((D, W_LRU))] + _lru_param_specs(),
        out_specs=(pl.BlockSpec((tt, D), row), pl.BlockSpec((1, SUB, D), seq), pl.BlockSpec((1, SUB, D), seq)),
        scratch_shapes=[pltpu.VMEM((SUB, D), f32), pltpu.VMEM((SUB, D), f32)],
        compiler_params=_cp(2),
    )(x, g, w, *params)


def _lru_sample(u, h0_rows, bufpad, params, tm):
    m = u.shape[0]
    row = lambda i: (i, 0)
    return pl.pallas_call(
        _lru_sample_kernel,
        out_shape=(jax.ShapeDtypeStruct((m, D), f32), jax.ShapeDtypeStruct((m, D), f32)),
        grid=(m // tm,),
        in_specs=[pl.BlockSpec((tm, W_LRU), row), pl.BlockSpec((tm, D), row), pl.BlockSpec((tm, D), row)]
        + _lru_param_specs(),
        out_specs=(pl.BlockSpec((tm, D), row), pl.BlockSpec((tm, D), row)),
        compiler_params=_cp(1),
    )(u, h0_rows, bufpad, *params)


def _gla_core(u, s_list, nb, c, wal_ref, bal_ref, gn_ref):
    r_rows = nb * c
    hk = GLA_HEADS * GLA_DK
    q = u[:, 0:hk]
    k = u[:, hk:2 * hk]
    v = u[:, 2 * hk:2 * hk + D]
    rg = u[:, 2 * hk + D:2 * hk + 2 * D]
    glow = u[:, 2 * hk + 2 * D:]
    z = _dot(glow, wal_ref[...]) + bal_ref[...]
    g = jax.nn.log_sigmoid(z) / GLA_TAU
    causal, same = _seq_masks(r_rows, c)
    g_hi, g_lo = _split(g)
    tril = causal.astype(bf16)
    ones_sq = same.astype(bf16)
    bcum = jnp.dot(tril, g_hi, preferred_element_type=f32) + jnp.dot(tril, g_lo, preferred_element_type=f32)
    btot = jnp.dot(ones_sq, g_hi, preferred_element_type=f32) + jnp.dot(ones_sq, g_lo, preferred_element_type=f32)
    qt = (q * GLA_DK ** -0.5) * jnp.exp(bcum)
    kt = k * jnp.exp(-bcum)
    ke = k * jnp.exp(btot - bcum)
    ones_v = jnp.ones((r_rows, GLA_DV), bf16)
    outs, new_s = [], []
    for h in range(GLA_HEADS):
        ks = slice(h * GLA_DK, (h + 1) * GLA_DK)
        vs = slice(h * GLA_DV, (h + 1) * GLA_DV)
        s_h = s_list[h]
        att = jnp.where(causal, _dot_nt(qt[:, ks], kt[:, ks]), 0.0)
        o_h = _dot(att, v[:, vs]) + _bdot(qt[:, ks], s_h, nb, c, "nn")
        blast = _bdot(g_hi[:, ks], ones_v, nb, c, "tn") + _bdot(g_lo[:, ks], ones_v, nb, c, "tn")
        new_s.append(jnp.exp(blast) * s_h + _bdot(ke[:, ks], v[:, vs], nb, c, "tn"))
        o_h = o_h * lax.rsqrt(jnp.mean(o_h * o_h, axis=-1, keepdims=True) + EPS) * gn_ref[...]
        outs.append(o_h)
    o = jnp.concatenate(outs, axis=1)
    return o * jax.nn.silu(rg), new_s


def _gla_prompt_kernel(x_ref, g_ref, w_ref, wal_ref, bal_ref, gn_ref, o_ref, s_ref, s_scr):
    t = pl.program_id(1)

    @pl.when(t == 0)
    def _():
        s_scr[...] = jnp.zeros_like(s_scr)

    u = _dot(_rms(x_ref[...], g_ref[...]), w_ref[...])
    s_list = [s_scr[h][None] for h in range(GLA_HEADS)]
    c = GLA_CHUNK
    for j in range(u.shape[0] // c):
        o, s_list = _gla_core(u[j * c:(j + 1) * c], s_list, 1, c, wal_ref, bal_ref, gn_ref)
        o_ref[j * c:(j + 1) * c, :] = o
    for h in range(GLA_HEADS):
        s_scr[h] = s_list[h][0]

    @pl.when(t == pl.num_programs(1) - 1)
    def _():
        s_ref[0] = s_scr[...]


def _gla_sample_kernel(u_ref, s0_ref, wal_ref, bal_ref, gn_ref, o_ref, s_ref, *, nb, c):
    s_list = [s0_ref[:, h] for h in range(GLA_HEADS)]
    o, new_s = _gla_core(u_ref[...], s_list, nb, c, wal_ref, bal_ref, gn_ref)
    o_ref[...] = o
    for h in range(GLA_HEADS):
        s_ref[:, h] = new_s[h]


def _gla_param_specs():
    hk = GLA_HEADS * GLA_DK
    return [_const_spec((LANE, hk)), _const_spec((1, hk)), _const_spec((1, GLA_DV))]


def _gla_prompt(x, g, w, params, nseq, seqlen, tt):
    nt = seqlen // tt
    row = lambda b, t: (b * nt + t, 0)
    return pl.pallas_call(
        _gla_prompt_kernel,
        out_shape=(jax.ShapeDtypeStruct((nseq * seqlen, D), f32),
                   jax.ShapeDtypeStruct((nseq, GLA_HEADS, GLA_DK, GLA_DV), f32)),
        grid=(nseq, nt),
        in_specs=[pl.BlockSpec((tt, D), row), _const_spec((1, D)), _const_spec((D, W_GLA))] + _gla_param_specs(),
        out_specs=(pl.BlockSpec((tt, D), row),
                   pl.BlockSpec((1, GLA_HEADS, GLA_DK, GLA_DV), lambda b, t: (b, 0, 0, 0))),
        scratch_shapes=[pltpu.VMEM((GLA_HEADS, GLA_DK, GLA_DV), f32)],
        compiler_params=_cp(2),
    )(x, g, w, *params)


def _gla_sample(u, s0, params, nb, c):
    nseq = s0.shape[0]
    sspec = pl.BlockSpec((nb, GLA_HEADS, GLA_DK, GLA_DV), lambda i: (i, 0, 0, 0))
    return pl.pallas_call(
        functools.partial(_gla_sample_kernel, nb=nb, c=c),
        out_shape=(jax.ShapeDtypeStruct((nseq * c, D), f32), jax.ShapeDtypeStruct(s0.shape, f32)),
        grid=(nseq // nb,),
        in_specs=[pl.BlockSpec((nb * c, W_GLA), lambda i: (i, 0)), sspec] + _gla_param_specs(),
        out_specs=(pl.BlockSpec((nb * c, D), lambda i: (i, 0)), sspec),
        compiler_params=_cp(1),
    )(u, s0, *params)


def _ssd_core(sz, xbc, sdt, h_list, nb, c, dtb_ref, alog_ref, dskip_ref, nrm_ref, exp_ref):
    r_rows = nb * c
    hp = SSD_HEADS // SSD_GROUPS * SSD_P
    sx = xbc[:, :D]
    dt = jax.nn.softplus(sdt + dtb_ref[...])
    da = dt * (-jnp.exp(alog_ref[...]))
    causal, same = _seq_masks(r_rows, c)
    cum = _dot_split(causal.astype(bf16), da)
    tot = _dot_split(same.astype(bf16), da)
    ecum = jnp.exp(cum)
    wgt = jnp.exp(tot - cum) * dt
    expd = _dot_split_rows(jnp.concatenate([ecum, wgt, da], axis=0), exp_ref[...])
    ecum_x = expd[0:r_rows]
    wgt_x = expd[r_rows:2 * r_rows]
    da_x = expd[2 * r_rows:]
    cum_t = cum.T
    dt_t = dt.T
    lane = lax.broadcasted_iota(jnp.int32, (r_rows, LANE), 1)
    xw = sx * wgt_x
    da_hi, da_lo = _split(da_x)
    ones_n = jnp.ones((r_rows, SSD_N), bf16)
    ys, new_h = [], []
    for grp in range(SSD_GROUPS):
        bm = xbc[:, D + grp * SSD_N:D + (grp + 1) * SSD_N]
        cm = xbc[:, D + (SSD_GROUPS + grp) * SSD_N:D + (SSD_GROUPS + grp + 1) * SSD_N]
        cb = _dot_nt(cm, bm)
        gs = slice(grp * hp, (grp + 1) * hp)
        y_inter = _bdot(cm, h_list[grp], nb, c, "nt") * ecum_x[:, gs]
        pairs = []
        for pr in range(hp // LANE):
            col = grp * hp + pr * LANE
            xp = sx[:, col:col + LANE]
            acc = None
            for half in range(2):
                hd = (col // SSD_P) + half
                seg = cum[:, hd:hd + 1] - cum_t[hd:hd + 1, :]
                m = cb * jnp.exp(jnp.where(causal, seg, -jnp.inf)) * dt_t[hd:hd + 1, :]
                xh = jnp.where((lane < SSD_P) if half == 0 else (lane >= SSD_P), xp, 0.0)
                part = _dot(m, xh)
                acc = part if acc is None else acc + part
            pairs.append(acc)
        ys.append(jnp.concatenate(pairs, axis=1) + y_inter)
        dec = jnp.exp(_bdot(da_hi[:, gs], ones_n, nb, c, "tn") + _bdot(da_lo[:, gs], ones_n, nb, c, "tn"))
        new_h.append(dec * h_list[grp] + _bdot(xw[:, gs], bm, nb, c, "tn"))
    y = jnp.concatenate(ys, axis=1) + dskip_ref[...] * sx
    y = y * jax.nn.silu(sz)
    yg = y.reshape(r_rows, SSD_GROUPS, hp)
    yg = yg * lax.rsqrt(jnp.mean(yg * yg, axis=-1, keepdims=True) + EPS)
    return yg.reshape(r_rows, D) * nrm_ref[...], new_h


def _dot_split_rows(x, m):
    hi, lo = _split(x)
    return jnp.dot(hi, m, preferred_element_type=f32) + jnp.dot(lo, m, preferred_element_type=f32)


def _ssd_prompt_kernel(x_ref, g_ref, w_ref, cw_ref, cb_ref, dtb_ref, alog_ref, dskip_ref, nrm_ref, exp_ref,
                       y_ref, h_ref, tail_ref, h_scr, tail_scr):
    t = pl.program_id(1)

    @pl.when(t == 0)
    def _():
        h_scr[...] = jnp.zeros_like(h_scr)
        tail_scr[...] = jnp.zeros_like(tail_scr)

    u = _dot(_rms(x_ref[...], g_ref[...]), w_ref[...])
    sz = u[:, :D]
    raw = u[:, D:D + SSD_CH]
    sdt = u[:, D + SSD_CH:]
    xbc = jax.nn.silu(_conv_carry(raw, tail_scr[...], cw_ref, cb_ref))
    h_list = [h_scr[grp][None] for grp in range(SSD_GROUPS)]
    c = SSD_CHUNK
    for j in range(u.shape[0] // c):
        rs = slice(j * c, (j + 1) * c)
        y, h_list = _ssd_core(sz[rs], xbc[rs], sdt[rs], h_list, 1, c,
                              dtb_ref, alog_ref, dskip_ref, nrm_ref, exp_ref)
        y_ref[rs, :] = y
    for grp in range(SSD_GROUPS):
        h_scr[grp] = h_list[grp][0]
    tail = raw[raw.shape[0] - SUB:]
    tail_scr[...] = tail
    tail_ref[0] = tail

    @pl.when(t == pl.num_programs(1) - 1)
    def _():
        h_ref[0] = h_scr[...]


def _ssd_sample_kernel(u_ref, bp_ref, h0_ref, cw_ref, cb_ref, dtb_ref, alog_ref, dskip_ref, nrm_ref, exp_ref,
                       y_ref, h_ref, *, nb, c):
    u = u_ref[...]
    sz = u[:, :D]
    raw = u[:, D:D + SSD_CH]
    sdt = u[:, D + SSD_CH:]
    xbc = jax.nn.silu(_conv_groups(raw, bp_ref[...], cw_ref, cb_ref))
    h_list = [h0_ref[:, grp] for grp in range(SSD_GROUPS)]
    y, new_h = _ssd_core(sz, xbc, sdt, h_list, nb, c, dtb_ref, alog_ref, dskip_ref, nrm_ref, exp_ref)
    y_ref[...] = y
    for grp in range(SSD_GROUPS):
        h_ref[:, grp] = new_h[grp]


def _ssd_param_specs():
    return [_const_spec((CONV_W, SSD_CH)), _const_spec((1, SSD_CH)), _const_spec((1, LANE)),
            _const_spec((1, LANE)), _const_spec((1, D)), _const_spec((1, D)), _const_spec((LANE, D))]


def _ssd_prompt(x, g, w, params, nseq, seqlen, tt):
    nt = seqlen // tt
    hp = SSD_HEADS // SSD_GROUPS * SSD_P
    row = lambda b, t: (b * nt + t, 0)
    return pl.pallas_call(
        _ssd_prompt_kernel,
        out_shape=(jax.ShapeDtypeStruct((nseq * seqlen, D), f32),
                   jax.ShapeDtypeStruct((nseq, SSD_GROUPS, hp, SSD_N), f32),
                   jax.ShapeDtypeStruct((nseq, SUB, SSD_CH), f32)),
        grid=(nseq, nt),
        in_specs=[pl.BlockSpec((tt, D), row), _const_spec((1, D)), _const_spec((D, W_SSD))] + _ssd_param_specs(),
        out_specs=(pl.BlockSpec((tt, D), row),
                   pl.BlockSpec((1, SSD_GROUPS, hp, SSD_N), lambda b, t: (b, 0, 0, 0)),
                   pl.BlockSpec((1, SUB, SSD_CH), lambda b, t: (b, 0, 0))),
        scratch_shapes=[pltpu.VMEM((SSD_GROUPS, hp, SSD_N), f32), pltpu.VMEM((SUB, SSD_CH), f32)],
        compiler_params=_cp(2),
    )(x, g, w, *params)


def _ssd_sample(u, bufpad, h0, params, nb, c):
    nseq = h0.shape[0]
    hp = SSD_HEADS // SSD_GROUPS * SSD_P
    hspec = pl.BlockSpec((nb, SSD_GROUPS, hp, SSD_N), lambda i: (i, 0, 0, 0))
    row = lambda i: (i, 0)
    return pl.pallas_call(
        functools.partial(_ssd_sample_kernel, nb=nb, c=c),
        out_shape=(jax.ShapeDtypeStruct((nseq * c, D), f32), jax.ShapeDtypeStruct(h0.shape, f32)),
        grid=(nseq // nb,),
        in_specs=[pl.BlockSpec((nb * c, W_SSD), row), pl.BlockSpec((nb * c, SSD_CH), row), hspec]
        + _ssd_param_specs(),
        out_specs=(pl.BlockSpec((nb * c, D), row), hspec),
        compiler_params=_cp(1),
    )(u, bufpad, h0, *params)


def _merge_kernel(x_ref, g_ref, wg_ref, yl_ref, yg_ref, ys_ref, wl_ref, wgl_ref, ws_ref, wm_ref, o_ref):
    x = x_ref[...]
    gates = jax.nn.sigmoid(_dot(_rms(x, g_ref[...]), wg_ref[...]))
    merged = gates[:, 0:D] * _dot(yl_ref[...], wl_ref[...])
    merged = merged + gates[:, D:2 * D] * _dot(yg_ref[...], wgl_ref[...])
    merged = merged + gates[:, 2 * D:] * _dot(ys_ref[...], ws_ref[...])
    o_ref[...] = x + _dot(merged, wm_ref[...])


def _merge(x, g, wgate, yl, yg, ys, wl, wgl, ws, wm, tm):
    m = x.shape[0]
    row = pl.BlockSpec((tm, D), lambda i: (i, 0))
    wsq = _const_spec((D, D))
    return pl.pallas_call(
        _merge_kernel,
        out_shape=jax.ShapeDtypeStruct((m, D), f32),
        grid=(m // tm,),
        in_specs=[row, _const_spec((1, D)), _const_spec((D, W_GATE)), row, row, row, wsq, wsq, wsq, wsq],
        out_specs=row,
        compiler_params=_cp(1),
    )(x, g, wgate, yl, yg, ys, wl, wgl, ws, wm)


def _xattn_kernel(x_ref, g_ref, wq_ref, k_ref, v_ref, wo_ref, o_ref, *, nb, rows):
    x = x_ref[...]
    q = _dot(_rms(x, g_ref[...]), wq_ref[...])
    outs = []
    for h in range(XA_HEADS):
        hs = slice(h * XA_HD, (h + 1) * XA_HD)
        s = _bdot(q[:, hs], k_ref[:, :, hs], nb, rows, "nt") * XA_HD ** -0.5
        s = s - jnp.max(s, axis=-1, keepdims=True)
        e = jnp.exp(s)
        p = e / jnp.sum(e, axis=-1, keepdims=True)
        outs.append(_bdot(p, v_ref[:, :, hs], nb, rows, "nn"))
    o_ref[...] = x + _dot(jnp.concatenate(outs, axis=1), wo_ref[...])


def _xattn(x, g, wq, k, v, wo, nb, rows, kv_block, kspec_idx, vspec_idx, grid, xidx):
    m = x.shape[0]
    row = pl.BlockSpec((nb * rows, D), xidx)
    return pl.pallas_call(
        functools.partial(_xattn_kernel, nb=nb, rows=rows),
        out_shape=jax.ShapeDtypeStruct((m, D), f32),
        grid=grid,
        in_specs=[row, _const_spec((1, D)), _const_spec((D, D)),
                  pl.BlockSpec(kv_block, kspec_idx), pl.BlockSpec(kv_block, vspec_idx), _const_spec((D, D))],
        out_specs=row,
        compiler_params=_cp(len(grid)),
    )(x, g, wq, k, v, wo)


FF_SPLIT = 2


def _ffn_kernel(x_ref, g_ref, wg_ref, wu_ref, wd_ref, o_ref):
    x = x_ref[...]
    h = _rms(x, g_ref[...]).astype(bf16)
    fc = D_FF // FF_SPLIT
    acc = x
    for j in range(FF_SPLIT):
        cs = slice(j * fc, (j + 1) * fc)
        gate = jnp.dot(h, wg_ref[:, cs], preferred_element_type=f32)
        up = jnp.dot(h, wu_ref[:, cs], preferred_element_type=f32)
        acc = acc + _dot(jax.nn.silu(gate) * up, wd_ref[cs, :])
    o_ref[...] = acc


def _ffn(x, g, wg, wu, wd, tm):
    m = x.shape[0]
    row = pl.BlockSpec((tm, D), lambda i: (i, 0))
    return pl.pallas_call(
        _ffn_kernel,
        out_shape=jax.ShapeDtypeStruct((m, D), f32),
        grid=(m // tm,),
        in_specs=[row, _const_spec((1, D)), _const_spec((D, D_FF)), _const_spec((D, D_FF)), _const_spec((D_FF, D))],
        out_specs=row,
        compiler_params=_cp(1),
    )(x, g, wg, wu, wd)


def _block_diag_tiles(w):
    nblk, bs, _ = w.shape
    per = LRU_TILE // bs
    w4 = w.reshape(nblk // per, per, bs, bs)
    eye = jnp.eye(per, dtype=w.dtype)
    t = jnp.einsum("tpij,pq->tpiqj", w4, eye)
    return t.reshape(nblk // per, LRU_TILE, LRU_TILE).astype(bf16)


def _pad_lanes(a, width):
    return jnp.pad(a, ((0, 0), (0, width - a.shape[1])))


def kernel(x_prompt, x_sample, mem_prompt, state_lru_h, state_lru_conv, state_gla_S, state_ssd_h, state_ssd_conv, cache_mem_k, cache_mem_v, norm_mix, w_in, lru_conv_w, lru_conv_b, lru_wa, lru_ba, lru_wx, lru_bx, lru_lambda, w_lru_out, gla_w_alpha, gla_b_alpha, gla_norm, w_gla_out, ssd_conv_w, ssd_conv_b, ssd_dt_bias, ssd_a_log, ssd_d, ssd_norm, w_ssd_out, w_mix_out, norm_xattn, norm_mem, w_xq, w_xk, w_xv, w_xo, norm_ffn, w_ffn_gate, w_ffn_up, w_ffn_down, norm_final):
    bp, lp, _ = x_prompt.shape
    bs, ls, _ = x_sample.shape
    assert ls == SUB, "sample kernels treat each 8-row group as one sequence"
    hk = GLA_HEADS * GLA_DK
    hp = SSD_HEADS // SSD_GROUPS * SSD_P
    xp = x_prompt.reshape(bp * lp, D)
    xs = x_sample.reshape(bs * ls, D)
    mem = mem_prompt.reshape(bp * MEM, D)
    row = lambda a: a.reshape(1, -1)
    expand = (jnp.arange(LANE)[:, None] == (jnp.arange(D)[None, :] // SSD_P)).astype(bf16)

    p_out = [[] for _ in range(7)]
    s_out = [[] for _ in range(5)]
    for l in range(DEPTH):
        wi = w_in[l]
        o0 = 2 * D
        o1 = o0 + 2 * hk + 2 * D + GLA_RANK
        o2 = o1 + D + SSD_CH + SSD_HEADS
        w_lru = wi[:, :o0].astype(bf16)
        w_gla = _pad_lanes(wi[:, o0:o1], W_GLA).astype(bf16)
        w_ssd = _pad_lanes(wi[:, o1:o2], W_SSD).astype(bf16)
        w_gate = wi[:, o2:].astype(bf16)
        g_mix = row(norm_mix[l])
        lru_params = (lru_conv_w[l], row(lru_conv_b[l]), row(lru_lambda[l]),
                      _block_diag_tiles(lru_wa[l]), row(lru_ba[l]), _block_diag_tiles(lru_wx[l]), row(lru_bx[l]))
        gla_params = (jnp.pad(gla_w_alpha[l], ((0, LANE - GLA_RANK), (0, 0))).astype(bf16),
                      row(gla_b_alpha[l]), row(gla_norm[l]))
        ssd_params = (ssd_conv_w[l], row(ssd_conv_b[l]), _pad_lanes(row(ssd_dt_bias[l]), LANE),
                      _pad_lanes(row(ssd_a_log[l]), LANE), row(jnp.repeat(ssd_d[l], SSD_P)),
                      row(ssd_norm[l]), expand)
        w_lo, w_go, w_so, w_mo = (w.astype(bf16) for w in (w_lru_out[l], w_gla_out[l], w_ssd_out[l], w_mix_out[l]))
        wq, wo = w_xq[l].astype(bf16), w_xo[l].astype(bf16)
        wkv = jnp.concatenate([w_xk[l], w_xv[l]], axis=1).astype(bf16)
        wfg, wfu, wfd = w_ffn_gate[l].astype(bf16), w_ffn_up[l].astype(bf16), w_ffn_down[l].astype(bf16)

        kv = _norm_matmul(mem, row(norm_mem[l]), wkv, 512)
        yl, p_h, p_tail = _lru_prompt(xp, g_mix, w_lru, lru_params, bp, lp, 256)
        yg, p_s = _gla_prompt(xp, g_mix, w_gla, gla_params, bp, lp, 256)
        ys, p_sh, p_stail = _ssd_prompt(xp, g_mix, w_ssd, ssd_params, bp, lp, 256)
        xp = _merge(xp, g_mix, w_gate, yl, yg, ys, w_lo, w_go, w_so, w_mo, 256)
        tq = 512
        ntq = lp // tq
        xp = _xattn(xp, row(norm_xattn[l]), wq, kv.reshape(bp, MEM, 2 * D), kv.reshape(bp, MEM, 2 * D), wo,
                    1, tq, (1, MEM, D), lambda b, t: (b, 0, 0), lambda b, t: (b, 0, 1), (bp, ntq),
                    lambda b, t: (b * ntq + t, 0))
        xp = _ffn(xp, row(norm_ffn[l]), wfg, wfu, wfd, 512)
        p_vals = (p_h[:, 0], p_tail[:, SUB - 3:], p_s, p_sh.reshape(bp, SSD_HEADS, SSD_P, SSD_N),
                  p_stail[:, SUB - 3:], kv[:, :D].reshape(bp, MEM, XA_HEADS, XA_HD),
                  kv[:, D:].reshape(bp, MEM, XA_HEADS, XA_HD))
        for lst, val in zip(p_out, p_vals):
            lst.append(val)

        ms = bs * ls
        u_lru = _norm_matmul(xs, g_mix, w_lru, 256)
        u_gla = _norm_matmul(xs, g_mix, w_gla, 256)
        u_ssd = _norm_matmul(xs, g_mix, w_ssd, 256)
        h0_rows = jnp.broadcast_to(state_lru_h[l][:, None, :], (bs, ls, D)).reshape(ms, D)
        pad5 = ((0, 0), (SUB - (CONV_W - 1), 0), (0, 0))
        bp_lru = jnp.pad(state_lru_conv[l], pad5).reshape(ms, D)
        bp_ssd = jnp.pad(state_ssd_conv[l], pad5).reshape(ms, SSD_CH)
        yl, h_all = _lru_sample(u_lru, h0_rows, bp_lru, lru_params, 256)
        yg, s_s = _gla_sample(u_gla, state_gla_S[l], gla_params, 8, ls)
        ys, s_sh = _ssd_sample(u_ssd, bp_ssd, state_ssd_h[l].reshape(bs, SSD_GROUPS, hp, SSD_N), ssd_params, 8, ls)
        xs = _merge(xs, g_mix, w_gate, yl, yg, ys, w_lo, w_go, w_so, w_mo, 256)
        nbx = 8
        xs = _xattn(xs, row(norm_xattn[l]), wq, cache_mem_k[l].reshape(bs, MEM, D),
                    cache_mem_v[l].reshape(bs, MEM, D), wo, nbx, ls, (nbx, MEM, D),
                    lambda i: (i, 0, 0), lambda i: (i, 0, 0), (bs // nbx,), lambda i: (i, 0))
        xs = _ffn(xs, row(norm_ffn[l]), wfg, wfu, wfd, 256)
        s_vals = (h_all.reshape(bs, ls, D)[:, ls - 1], u_lru.reshape(bs, ls, W_LRU)[:, ls - 3:, :D], s_s,
                  s_sh.reshape(bs, SSD_HEADS, SSD_P, SSD_N),
                  u_ssd.reshape(bs, ls, W_SSD)[:, ls - 3:, D:D + SSD_CH])
        for lst, val in zip(s_out, s_vals):
            lst.append(val)

    y_prompt = _final_norm(xp, row(norm_final), 512).reshape(bp, lp, D)
    y_sample = _final_norm(xs, row(norm_final), 256).reshape(bs, ls, D)
    p_stack = [jnp.stack(t, axis=0) for t in p_out]
    s_stack = [jnp.stack(t, axis=0) for t in s_out]
    return (y_prompt, y_sample, *p_stack, *s_stack)
```

```python
import functools

import jax
import jax.numpy as jnp
from jax import lax
from jax.experimental import pallas as pl
from jax.experimental.pallas import tpu as pltpu

f32 = jnp.float32
bf16 = jnp.bfloat16

D = 1024
DEPTH = 4
EPS = 1e-6
CONV_W = 4
LRU_C = 8.0
LRU_TILE = 256
GLA_HEADS, GLA_DK, GLA_DV, GLA_RANK, GLA_TAU, GLA_CHUNK = 4, 128, 256, 16, 16.0, 64
SSD_HEADS, SSD_P, SSD_GROUPS, SSD_N, SSD_CHUNK = 16, 64, 2, 64, 128
SSD_HP = SSD_HEADS // SSD_GROUPS * SSD_P
SSD_CH = D + 2 * SSD_GROUPS * SSD_N
XA_HEADS, XA_HD, MEM = 4, 256, 256
D_FF = 2816
LANE = 128
SUB = 8
W_LRU, W_GLA, W_SSD, W_GATE = 2 * D, 3 * D + LANE, D + SSD_CH + LANE, 3 * D
W_MIX = W_LRU + W_GLA + W_SSD
O_GLA, O_SSD = W_LRU, W_LRU + W_GLA
VMEM_LIMIT = 56 * 1024 * 1024

T_MIX = 256
T_ROW = 512
T_MERGE = 256
T_SAMPLE = 256
NB_STATE = 8


def _cp(n_axes):
    return pltpu.CompilerParams(dimension_semantics=("arbitrary",) * n_axes,
                                vmem_limit_bytes=VMEM_LIMIT)


def _const_spec(shape):
    nd = len(shape)
    return pl.BlockSpec(shape, lambda *_: (0,) * nd, pipeline_mode=pl.Buffered(1))


def _layer_spec(shape, layer):
    nd = len(shape)
    return pl.BlockSpec((None,) + tuple(shape), lambda *_: (layer,) + (0,) * nd,
                        pipeline_mode=pl.Buffered(1))


def _rms(x, g):
    return x * lax.rsqrt(jnp.mean(x * x, axis=-1, keepdims=True) + EPS) * g


def _dot(a, b):
    return jnp.dot(a.astype(bf16), b.astype(bf16), preferred_element_type=f32)


def _dot_nt(a, b):
    return lax.dot_general(a.astype(bf16), b.astype(bf16), (((1,), (1,)), ((), ())),
                           preferred_element_type=f32)


def _split(x):
    hi = x.astype(bf16)
    return hi, (x - hi.astype(f32)).astype(bf16)


def _dot_split(m, x):
    hi, lo = _split(x)
    return (jnp.dot(m, hi, preferred_element_type=f32) + jnp.dot(m, lo, preferred_element_type=f32))


def _dot_split_rows(x, m):
    hi, lo = _split(x)
    return jnp.dot(hi, m, preferred_element_type=f32) + jnp.dot(lo, m, preferred_element_type=f32)


def _bdot(a, b, nb, rows, mode):
    a = a.astype(bf16)
    b = b.astype(bf16)
    if mode == "tn":
        if nb == 1:
            return lax.dot_general(a, b, (((0,), (0,)), ((), ())), preferred_element_type=f32)[None]
        a3 = a.reshape(nb, rows, a.shape[-1])
        b3 = b.reshape(nb, rows, b.shape[-1])
        return jnp.einsum("bki,bkj->bij", a3, b3, preferred_element_type=f32)
    if nb == 1:
        if mode == "nn":
            return jnp.dot(a, b[0], preferred_element_type=f32)
        return lax.dot_general(a, b[0], (((1,), (1,)), ((), ())), preferred_element_type=f32)
    a3 = a.reshape(nb, rows, a.shape[-1])
    eq = "bik,bkj->bij" if mode == "nn" else "bik,bjk->bij"
    out = jnp.einsum(eq, a3, b, preferred_element_type=f32)
    return out.reshape(nb * rows, out.shape[-1])


def _seq_masks(r, c):
    ri = lax.broadcasted_iota(jnp.int32, (r, r), 0)
    ci = lax.broadcasted_iota(jnp.int32, (r, r), 1)
    if r == c:
        same = jnp.full((r, r), True)
    else:
        shift = c.bit_length() - 1
        same = (ri >> shift) == (ci >> shift)
    causal = same & (ri >= ci)
    return causal, same


def _conv_carry(x, tail, w_ref, b_ref):
    xe = jnp.concatenate([tail, x], axis=0)
    y = b_ref[...] + w_ref[0:1, :] * pltpu.roll(xe, 3, 0)[SUB:]
    y = y + w_ref[1:2, :] * pltpu.roll(xe, 2, 0)[SUB:]
    y = y + w_ref[2:3, :] * pltpu.roll(xe, 1, 0)[SUB:]
    return y + w_ref[3:4, :] * x


def _conv_groups(x, bufpad, w_ref, b_ref):
    rows = x.shape[0]
    l = lax.broadcasted_iota(jnp.int32, (rows, 1), 0) & (SUB - 1)
    y = b_ref[...]
    for k in range(CONV_W - 1):
        s = CONV_W - 1 - k
        sh = jnp.where(l >= s, pltpu.roll(x, s, 0), pltpu.roll(bufpad, rows - SUB + s, 0))
        y = y + w_ref[k:k + 1, :] * sh
    return y + w_ref[3:4, :] * x


def _norm_matmul_kernel(x_ref, g_ref, w_ref, o_ref):
    o_ref[...] = _dot(_rms(x_ref[...], g_ref[...]), w_ref[...])


def _norm_matmul(name, x, g, w, layer, tm):
    m, k = x.shape
    n = w.shape[-1]
    return pl.pallas_call(
        _norm_matmul_kernel,
        name=name,
        out_shape=jax.ShapeDtypeStruct((m, n), f32),
        grid=(m // tm,),
        in_specs=[pl.BlockSpec((tm, k), lambda i: (i, 0)), _layer_spec((1, k), layer), _layer_spec((k, n), layer)],
        out_specs=pl.BlockSpec((tm, n), lambda i: (i, 0)),
        compiler_params=_cp(1),
    )(x, g, w)


def _final_norm_kernel(x_ref, g_ref, o_ref):
    o_ref[...] = _rms(x_ref[...], g_ref[...])


def _final_norm(name, x, g, tm):
    m, k = x.shape
    return pl.pallas_call(
        _final_norm_kernel,
        name=name,
        out_shape=jax.ShapeDtypeStruct((m, k), f32),
        grid=(m // tm,),
        in_specs=[pl.BlockSpec((tm, k), lambda i: (i, 0)), _const_spec((1, k))],
        out_specs=pl.BlockSpec((tm, k), lambda i: (i, 0)),
        compiler_params=_cp(1),
    )(x, g)


def _lru_math(xc, lam_ref, wa_ref, ba_ref, wx_ref, bx_ref):
    xcb = xc.astype(bf16)
    nt = D // LRU_TILE
    r_pre = jnp.concatenate([jnp.dot(xcb[:, i * LRU_TILE:(i + 1) * LRU_TILE], wa_ref[i],
                                     preferred_element_type=f32) for i in range(nt)], axis=1)
    i_pre = jnp.concatenate([jnp.dot(xcb[:, i * LRU_TILE:(i + 1) * LRU_TILE], wx_ref[i],
                                     preferred_element_type=f32) for i in range(nt)], axis=1)
    r = jax.nn.sigmoid(r_pre + ba_ref[...])
    ig = jax.nn.sigmoid(i_pre + bx_ref[...])
    log_a = -LRU_C * r * jax.nn.softplus(-lam_ref[...])
    a = jnp.exp(log_a)
    t = jnp.tanh(log_a)
    mult = jnp.sqrt(-2.0 * t / (1.0 - t))
    return a, mult * (ig * xc)


def _group_scan(a, b):
    l = lax.broadcasted_iota(jnp.int32, (a.shape[0], 1), 0) & (SUB - 1)
    for s in (1, 2, 4):
        m = l >= s
        b = jnp.where(m, a * pltpu.roll(b, s, 0) + b, b)
        a = jnp.where(m, a * pltpu.roll(a, s, 0), a)
    return a, b


def _lru_block(u, tail, carry, p):
    cw_ref, cb_ref, lam_ref, wa_ref, ba_ref, wx_ref, bx_ref = p
    x = u[:, :D]
    gate = u[:, D:]
    xc = _conv_carry(x, tail, cw_ref, cb_ref)
    a, b = _lru_math(xc, lam_ref, wa_ref, ba_ref, wx_ref, bx_ref)
    a, b = _group_scan(a, b)
    hs = []
    for j in range(x.shape[0] // SUB):
        hj = a[j * SUB:(j + 1) * SUB] * carry + b[j * SUB:(j + 1) * SUB]
        carry = hj[SUB - 1:SUB]
        hs.append(hj)
    h = jnp.concatenate(hs, axis=0)
    return h * jax.nn.gelu(gate), carry


def _lru_sample_kernel(u_ref, h0_ref, bp_ref, cw_ref, cb_ref, lam_ref, wa_ref, ba_ref, wx_ref, bx_ref,
                       y_ref, h_ref):
    x = u_ref[:, 0:D]
    gate = u_ref[:, D:W_LRU]
    xc = _conv_groups(x, bp_ref[...], cw_ref, cb_ref)
    a, b = _lru_math(xc, lam_ref, wa_ref, ba_ref, wx_ref, bx_ref)
    a, b = _group_scan(a, b)
    nseq = x.shape[0] // SUB
    h3 = a.reshape(nseq, SUB, D) * h0_ref[...] + b.reshape(nseq, SUB, D)
    h_ref[...] = h3[:, SUB - 1:SUB, :]
    y_ref[...] = h3.reshape(nseq * SUB, D) * jax.nn.gelu(gate)


def _lru_param_specs(layer):
    nt = D // LRU_TILE
    return [_layer_spec((CONV_W, D), layer), _layer_spec((1, D), layer), _layer_spec((1, D), layer),
            _layer_spec((nt, LRU_TILE, LRU_TILE), layer), _layer_spec((1, D), layer),
            _layer_spec((nt, LRU_TILE, LRU_TILE), layer), _layer_spec((1, D), layer)]


def _lru_sample(u, h0, bufpad, params, layer, tm):
    m = u.shape[0]
    row = lambda i: (i, 0)
    seq3 = lambda i: (i, 0, 0)
    return pl.pallas_call(
        _lru_sample_kernel,
        name="lru_sample",
        out_shape=(jax.ShapeDtypeStruct((m, D), f32), jax.ShapeDtypeStruct((m // SUB, 1, D), f32)),
        grid=(m // tm,),
        in_specs=[pl.BlockSpec((tm, W_MIX), row),
                  pl.BlockSpec((None, tm // SUB, 1, D), lambda i: (layer, i, 0, 0)),
                  pl.BlockSpec((tm, D), row)] + _lru_param_specs(layer),
        out_specs=(pl.BlockSpec((tm, D), row), pl.BlockSpec((tm // SUB, 1, D), seq3)),
        compiler_params=_cp(1),
    )(u, h0, bufpad, *params)


def _gla_core(u, s_list, nb, c, p):
    wal_ref, bal_ref, gn_ref = p
    r_rows = nb * c
    hk = GLA_HEADS * GLA_DK
    q = u[:, 0:hk]
    k = u[:, hk:2 * hk]
    v = u[:, 2 * hk:2 * hk + D]
    rg = u[:, 2 * hk + D:2 * hk + 2 * D]
    glow = u[:, 2 * hk + 2 * D:]
    z = _dot(glow, wal_ref[...]) + bal_ref[...]
    g = jax.nn.log_sigmoid(z) / GLA_TAU
    causal, same = _seq_masks(r_rows, c)
    g_hi, g_lo = _split(g)
    tril = causal.astype(bf16)
    ones_sq = same.astype(bf16)
    bcum = jnp.dot(tril, g_hi, preferred_element_type=f32) + jnp.dot(tril, g_lo, preferred_element_type=f32)
    btot = jnp.dot(ones_sq, g_hi, preferred_element_type=f32) + jnp.dot(ones_sq, g_lo, preferred_element_type=f32)
    qt = (q * GLA_DK ** -0.5) * jnp.exp(bcum)
    kt = k * jnp.exp(-bcum)
    ke = k * jnp.exp(btot - bcum)
    ones_v = jnp.ones((r_rows, GLA_DV), bf16)
    outs, new_s = [], []
    for h in range(GLA_HEADS):
        ks = slice(h * GLA_DK, (h + 1) * GLA_DK)
        vs = slice(h * GLA_DV, (h + 1) * GLA_DV)
        s_h = s_list[h]
        att = jnp.where(causal, _dot_nt(qt[:, ks], kt[:, ks]), 0.0)
        o_h = _dot(att, v[:, vs]) + _bdot(qt[:, ks], s_h, nb, c, "nn")
        blast = _bdot(g_hi[:, ks], ones_v, nb, c, "tn") + _bdot(g_lo[:, ks], ones_v, nb, c, "tn")
        new_s.append(jnp.exp(blast) * s_h + _bdot(ke[:, ks], v[:, vs], nb, c, "tn"))
        o_h = o_h * lax.rsqrt(jnp.mean(o_h * o_h, axis=-1, keepdims=True) + EPS) * gn_ref[...]
        outs.append(o_h)
    o = jnp.concatenate(outs, axis=1)
    return o * jax.nn.silu(rg), new_s


def _gla_sample_kernel(u_ref, s0_ref, wal_ref, bal_ref, gn_ref, o_ref, s_ref, *, nb, c):
    s_list = [s0_ref[:, h] for h in range(GLA_HEADS)]
    o, new_s = _gla_core(u_ref[:, O_GLA:O_GLA + W_GLA], s_list, nb, c, (wal_ref, bal_ref, gn_ref))
    o_ref[...] = o
    for h in range(GLA_HEADS):
        s_ref[:, h] = new_s[h]


def _gla_param_specs(layer):
    hk = GLA_HEADS * GLA_DK
    return [_layer_spec((LANE, hk), layer), _layer_spec((1, hk), layer), _layer_spec((1, GLA_DV), layer)]


def _gla_sample(u, s0_all, params, layer, nb, c):
    nseq = s0_all.shape[1]
    blk = (nb, GLA_HEADS, GLA_DK, GLA_DV)
    return pl.pallas_call(
        functools.partial(_gla_sample_kernel, nb=nb, c=c),
        name="gla_sample",
        out_shape=(jax.ShapeDtypeStruct((nseq * c, D), f32), jax.ShapeDtypeStruct(s0_all.shape[1:], f32)),
        grid=(nseq // nb,),
        in_specs=[pl.BlockSpec((nb * c, W_MIX), lambda i: (i, 0)),
                  pl.BlockSpec((None,) + blk, lambda i: (layer, i, 0, 0, 0))] + _gla_param_specs(layer),
        out_specs=(pl.BlockSpec((nb * c, D), lambda i: (i, 0)), pl.BlockSpec(blk, lambda i: (i, 0, 0, 0))),
        compiler_params=_cp(1),
    )(u, s0_all, *params)


def _ssd_core(sz, xbc, sdt, h_list, nb, c, p):
    dtb_ref, alog_ref, dskip_ref, nrm_ref, exp_ref = p
    r_rows = nb * c
    sx = xbc[:, :D]
    dt = jax.nn.softplus(sdt + dtb_ref[...])
    da = dt * (-jnp.exp(alog_ref[...]))
    causal, same = _seq_masks(r_rows, c)
    cum = _dot_split(causal.astype(bf16), da)
    tot = _dot_split(same.astype(bf16), da)
    ecum = jnp.exp(cum)
    wgt = jnp.exp(tot - cum) * dt
    expd = _dot_split_rows(jnp.concatenate([ecum, wgt, da], axis=0), exp_ref[...])
    ecum_x = expd[0:r_rows]
    wgt_x = expd[r_rows:2 * r_rows]
    da_x = expd[2 * r_rows:]
    cum_t = cum.T
    dt_t = dt.T
    lane = lax.broadcasted_iota(jnp.int32, (r_rows, LANE), 1)
    xw = sx * wgt_x
    da_hi, da_lo = _split(da_x)
    ones_n = jnp.ones((r_rows, SSD_N), bf16)
    ys, new_h = [], []
    for grp in range(SSD_GROUPS):
        bm = xbc[:, D + grp * SSD_N:D + (grp + 1) * SSD_N]
        cm = xbc[:, D + (SSD_GROUPS + grp) * SSD_N:D + (SSD_GROUPS + grp + 1) * SSD_N]
        cb = _dot_nt(cm, bm)
        gs = slice(grp * SSD_HP, (grp + 1) * SSD_HP)
        y_inter = _bdot(cm, h_list[grp], nb, c, "nt") * ecum_x[:, gs]
        pairs = []
        for pr in range(SSD_HP // LANE):
            col = grp * SSD_HP + pr * LANE
            xp = sx[:, col:col + LANE]
            acc = None
            for half in range(2):
                hd = (col // SSD_P) + half
                seg = cum[:, hd:hd + 1] - cum_t[hd:hd + 1, :]
                m = cb * jnp.exp(jnp.where(causal, seg, -jnp.inf)) * dt_t[hd:hd + 1, :]
                xh = jnp.where((lane < SSD_P) if half == 0 else (lane >= SSD_P), xp, 0.0)
                part = _dot(m, xh)
                acc = part if acc is None else acc + part
            pairs.append(acc)
        ys.append(jnp.concatenate(pairs, axis=1) + y_inter)
        dec = jnp.exp(_bdot(da_hi[:, gs], ones_n, nb, c, "tn") + _bdot(da_lo[:, gs], ones_n, nb, c, "tn"))
        new_h.append(dec * h_list[grp] + _bdot(xw[:, gs], bm, nb, c, "tn"))
    y = jnp.concatenate(ys, axis=1) + dskip_ref[...] * sx
    y = y * jax.nn.silu(sz)
    yg = y.reshape(r_rows, SSD_GROUPS, SSD_HP)
    yg = yg * lax.rsqrt(jnp.mean(yg * yg, axis=-1, keepdims=True) + EPS)
    return yg.reshape(r_rows, D) * nrm_ref[...], new_h


def _ssd_sample_kernel(u_ref, bp_ref, h0_ref, cw_ref, cb_ref, dtb_ref, alog_ref, dskip_ref, nrm_ref, exp_ref,
                       y_ref, h_ref, *, nb, c):
    sz = u_ref[:, O_SSD:O_SSD + D]
    raw = u_ref[:, O_SSD + D:O_SSD + D + SSD_CH]
    sdt = u_ref[:, O_SSD + D + SSD_CH:O_SSD + W_SSD]
    xbc = jax.nn.silu(_conv_groups(raw, bp_ref[...], cw_ref, cb_ref))
    hpg = SSD_HEADS // SSD_GROUPS
    h_list = [h0_ref[:, grp * hpg:(grp + 1) * hpg].reshape(nb, SSD_HP, SSD_N) for grp in range(SSD_GROUPS)]
    y, new_h = _ssd_core(sz, xbc, sdt, h_list, nb, c, (dtb_ref, alog_ref, dskip_ref, nrm_ref, exp_ref))
    y_ref[...] = y
    for grp in range(SSD_GROUPS):
        h_ref[:, grp * hpg:(grp + 1) * hpg] = new_h[grp].reshape(nb, hpg, SSD_P, SSD_N)


def _ssd_param_specs(layer):
    return [_layer_spec((CONV_W, SSD_CH), layer), _layer_spec((1, SSD_CH), layer), _layer_spec((1, LANE), layer),
            _layer_spec((1, LANE), layer), _layer_spec((1, D), layer), _layer_spec((1, D), layer),
            _const_spec((LANE, D))]


def _ssd_sample(u, bufpad, h0_all, params, layer, nb, c):
    nseq = h0_all.shape[1]
    blk = (nb, SSD_HEADS, SSD_P, SSD_N)
    row = lambda i: (i, 0)
    return pl.pallas_call(
        functools.partial(_ssd_sample_kernel, nb=nb, c=c),
        name="ssd_sample",
        out_shape=(jax.ShapeDtypeStruct((nseq * c, D), f32), jax.ShapeDtypeStruct(h0_all.shape[1:], f32)),
        grid=(nseq // nb,),
        in_specs=[pl.BlockSpec((nb * c, W_MIX), row), pl.BlockSpec((nb * c, SSD_CH), row),
                  pl.BlockSpec((None,) + blk, lambda i: (layer, i, 0, 0, 0))] + _ssd_param_specs(layer),
        out_specs=(pl.BlockSpec((nb * c, D), row), pl.BlockSpec(blk, lambda i: (i, 0, 0, 0))),
        compiler_params=_cp(1),
    )(u, bufpad, h0_all, *params)


def _mix_prompt_kernel(x_ref, g_ref, w_ref, *refs):
    lru_p = refs[0:7]
    gla_p = refs[7:10]
    ssd_cw, ssd_cb = refs[10:12]
    ssd_p = refs[12:17]
    yl_ref, yg_ref, ys_ref, lh_ref, ltail_ref, gs_ref, sh_ref, stail_ref = refs[17:25]
    hc_scr, ltail_scr, s_scr, h_scr, stail_scr = refs[25:30]
    t = pl.program_id(1)

    @pl.when(t == 0)
    def _():
        hc_scr[...] = jnp.zeros_like(hc_scr)
        ltail_scr[...] = jnp.zeros_like(ltail_scr)
        s_scr[...] = jnp.zeros_like(s_scr)
        h_scr[...] = jnp.zeros_like(h_scr)
        stail_scr[...] = jnp.zeros_like(stail_scr)

    hn = _rms(x_ref[...], g_ref[...]).astype(bf16)
    rows = hn.shape[0]

    u = jnp.dot(hn, w_ref[:, 0:W_LRU], preferred_element_type=f32)
    y, carry = _lru_block(u, ltail_scr[...], hc_scr[0:1, :], lru_p)
    yl_ref[...] = y
    hc = jnp.broadcast_to(carry, (SUB, D))
    hc_scr[...] = hc
    lh_ref[0] = hc
    ltail = u[rows - SUB:, :D]
    ltail_scr[...] = ltail
    ltail_ref[0] = ltail

    u = jnp.dot(hn, w_ref[:, O_GLA:O_GLA + W_GLA], preferred_element_type=f32)
    s_list = [s_scr[h][None] for h in range(GLA_HEADS)]
    c = GLA_CHUNK
    for j in range(rows // c):
        o, s_list = _gla_core(u[j * c:(j + 1) * c], s_list, 1, c, gla_p)
        yg_ref[j * c:(j + 1) * c, :] = o
    for h in range(GLA_HEADS):
        s_scr[h] = s_list[h][0]

    u = jnp.dot(hn, w_ref[:, O_SSD:O_SSD + W_SSD], preferred_element_type=f32)
    sz = u[:, :D]
    raw = u[:, D:D + SSD_CH]
    sdt = u[:, D + SSD_CH:]
    xbc = jax.nn.silu(_conv_carry(raw, stail_scr[...], ssd_cw, ssd_cb))
    h_list = [h_scr[grp][None] for grp in range(SSD_GROUPS)]
    c = SSD_CHUNK
    for j in range(rows // c):
        rs = slice(j * c, (j + 1) * c)
        y, h_list = _ssd_core(sz[rs], xbc[rs], sdt[rs], h_list, 1, c, ssd_p)
        ys_ref[rs, :] = y
    for grp in range(SSD_GROUPS):
        h_scr[grp] = h_list[grp][0]
    stail = raw[rows - SUB:]
    stail_scr[...] = stail
    stail_ref[0] = stail

    @pl.when(t == pl.num_programs(1) - 1)
    def _():
        gs_ref[0] = s_scr[...]
        hpg = SSD_HEADS // SSD_GROUPS
        for grp in range(SSD_GROUPS):
            sh_ref[0, grp * hpg:(grp + 1) * hpg] = h_scr[grp].reshape(hpg, SSD_P, SSD_N)


def _mix_prompt(x, g, w, lru_params, gla_params, ssd_params, layer, nseq, seqlen, tt):
    nt = seqlen // tt
    row = lambda b, t: (b * nt + t, 0)
    seq3 = lambda b, t: (b, 0, 0)
    seq4 = lambda b, t: (b, 0, 0, 0)
    rows = nseq * seqlen
    return pl.pallas_call(
        _mix_prompt_kernel,
        name="mix_prompt",
        out_shape=(jax.ShapeDtypeStruct((rows, D), f32), jax.ShapeDtypeStruct((rows, D), f32),
                   jax.ShapeDtypeStruct((rows, D), f32),
                   jax.ShapeDtypeStruct((nseq, SUB, D), f32), jax.ShapeDtypeStruct((nseq, SUB, D), f32),
                   jax.ShapeDtypeStruct((nseq, GLA_HEADS, GLA_DK, GLA_DV), f32),
                   jax.ShapeDtypeStruct((nseq, SSD_HEADS, SSD_P, SSD_N), f32),
                   jax.ShapeDtypeStruct((nseq, SUB, SSD_CH), f32)),
        grid=(nseq, nt),
        in_specs=[pl.BlockSpec((tt, D), row), _layer_spec((1, D), layer), _layer_spec((D, W_MIX), layer)]
        + _lru_param_specs(layer) + _gla_param_specs(layer) + _ssd_param_specs(layer),
        out_specs=(pl.BlockSpec((tt, D), row), pl.BlockSpec((tt, D), row), pl.BlockSpec((tt, D), row),
                   pl.BlockSpec((1, SUB, D), seq3), pl.BlockSpec((1, SUB, D), seq3),
                   pl.BlockSpec((1, GLA_HEADS, GLA_DK, GLA_DV), seq4),
                   pl.BlockSpec((1, SSD_HEADS, SSD_P, SSD_N), seq4),
                   pl.BlockSpec((1, SUB, SSD_CH), seq3)),
        scratch_shapes=[pltpu.VMEM((SUB, D), f32), pltpu.VMEM((SUB, D), f32),
                        pltpu.VMEM((GLA_HEADS, GLA_DK, GLA_DV), f32),
                        pltpu.VMEM((SSD_GROUPS, SSD_HP, SSD_N), f32), pltpu.VMEM((SUB, SSD_CH), f32)],
        compiler_params=_cp(2),
    )(x, g, w, *lru_params, *gla_params, *ssd_params)


def _merge_kernel(x_ref, g_ref, wg_ref, yl_ref, yg_ref, ys_ref, wl_ref, wgl_ref, ws_ref, wm_ref, o_ref):
    x = x_ref[...]
    gates = jax.nn.sigmoid(_dot(_rms(x, g_ref[...]), wg_ref[...]))
    merged = gates[:, 0:D] * _dot(yl_ref[...], wl_ref[...])
    merged = merged + gates[:, D:2 * D] * _dot(yg_ref[...], wgl_ref[...])
    merged = merged + gates[:, 2 * D:] * _dot(ys_ref[...], ws_ref[...])
    o_ref[...] = x + _dot(merged, wm_ref[...])


def _merge(name, x, g, wgate, yl, yg, ys, wl, wgl, ws, wm, layer, tm):
    m = x.shape[0]
    row = pl.BlockSpec((tm, D), lambda i: (i, 0))
    wsq = _layer_spec((D, D), layer)
    return pl.pallas_call(
        _merge_kernel,
        name=name,
        out_shape=jax.ShapeDtypeStruct((m, D), f32),
        grid=(m // tm,),
        in_specs=[row, _layer_spec((1, D), layer), _layer_spec((D, W_GATE), layer), row, row, row,
                  wsq, wsq, wsq, wsq],
        out_specs=row,
        compiler_params=_cp(1),
    )(x, g, wgate, yl, yg, ys, wl, wgl, ws, wm)


def _attend(q, get_k, get_v, nb, rows):
    outs = []
    for h in range(XA_HEADS):
        hs = slice(h * XA_HD, (h + 1) * XA_HD)
        s = _bdot(q[:, hs], get_k(h), nb, rows, "nt") * XA_HD ** -0.5
        s = s - jnp.max(s, axis=-1, keepdims=True)
        e = jnp.exp(s)
        p = e / jnp.sum(e, axis=-1, keepdims=True)
        outs.append(_bdot(p, get_v(h), nb, rows, "nn"))
    return jnp.concatenate(outs, axis=1)


def _xattn_prompt_kernel(x_ref, g_ref, wq_ref, k_ref, v_ref, wo_ref, o_ref):
    x = x_ref[...]
    q = _dot(_rms(x, g_ref[...]), wq_ref[...])
    o = _attend(q, lambda h: k_ref[:, :, h * XA_HD:(h + 1) * XA_HD],
                lambda h: v_ref[:, :, h * XA_HD:(h + 1) * XA_HD], 1, x.shape[0])
    o_ref[...] = x + _dot(o, wo_ref[...])


def _xattn_sample_kernel(x_ref, g_ref, wq_ref, k_ref, v_ref, wo_ref, o_ref, *, nb, rows):
    x = x_ref[...]
    q = _dot(_rms(x, g_ref[...]), wq_ref[...])
    o = _attend(q, lambda h: k_ref[:, :, h, :], lambda h: v_ref[:, :, h, :], nb, rows)
    o_ref[...] = x + _dot(o, wo_ref[...])


def _xattn_prompt(x, g, wq, kv, wo, layer, nseq, seqlen, tq):
    nt = seqlen // tq
    row = pl.BlockSpec((tq, D), lambda b, t: (b * nt + t, 0))
    return pl.pallas_call(
        _xattn_prompt_kernel,
        name="xattn_prompt",
        out_shape=jax.ShapeDtypeStruct(x.shape, f32),
        grid=(nseq, nt),
        in_specs=[row, _layer_spec((1, D), layer), _layer_spec((D, D), layer),
                  pl.BlockSpec((1, MEM, D), lambda b, t: (b, 0, 0)),
                  pl.BlockSpec((1, MEM, D), lambda b, t: (b, 0, 1)), _layer_spec((D, D), layer)],
        out_specs=row,
        compiler_params=_cp(2),
    )(x, g, wq, kv, kv, wo)


def _xattn_sample(x, g, wq, k_all, v_all, wo, layer, nb, rows):
    nseq = k_all.shape[1]
    row = pl.BlockSpec((nb * rows, D), lambda i: (i, 0))
    kvspec = pl.BlockSpec((None, nb, MEM, XA_HEADS, XA_HD), lambda i: (layer, i, 0, 0, 0))
    return pl.pallas_call(
        functools.partial(_xattn_sample_kernel, nb=nb, rows=rows),
        name="xattn_sample",
        out_shape=jax.ShapeDtypeStruct(x.shape, f32),
        grid=(nseq // nb,),
        in_specs=[row, _layer_spec((1, D), layer), _layer_spec((D, D), layer), kvspec, kvspec,
                  _layer_spec((D, D), layer)],
        out_specs=row,
        compiler_params=_cp(1),
    )(x, g, wq, k_all, v_all, wo)


FF_SPLIT = 2


def _ffn_kernel(x_ref, g_ref, wg_ref, wu_ref, wd_ref, o_ref):
    x = x_ref[...]
    h = _rms(x, g_ref[...]).astype(bf16)
    fc = D_FF // FF_SPLIT
    acc = x
    for j in range(FF_SPLIT):
        cs = slice(j * fc, (j + 1) * fc)
        gate = jnp.dot(h, wg_ref[:, cs], preferred_element_type=f32)
        up = jnp.dot(h, wu_ref[:, cs], preferred_element_type=f32)
        acc = acc + _dot(jax.nn.silu(gate) * up, wd_ref[cs, :])
    o_ref[...] = acc


def _ffn(name, x, g, wg, wu, wd, layer, tm):
    m = x.shape[0]
    row = pl.BlockSpec((tm, D), lambda i: (i, 0))
    return pl.pallas_call(
        _ffn_kernel,
        name=name,
        out_shape=jax.ShapeDtypeStruct((m, D), f32),
        grid=(m // tm,),
        in_specs=[row, _layer_spec((1, D), layer), _layer_spec((D, D_FF), layer), _layer_spec((D, D_FF), layer),
                  _layer_spec((D_FF, D), layer)],
        out_specs=row,
        compiler_params=_cp(1),
    )(x, g, wg, wu, wd)


def _block_diag_tiles(w):
    depth, nblk, bs, _ = w.shape
    per = LRU_TILE // bs
    w4 = w.reshape(depth, nblk // per, per, bs, bs)
    eye = jnp.eye(per, dtype=w.dtype)
    t = jnp.einsum("ltpij,pq->ltpiqj", w4, eye)
    return t.reshape(depth, nblk // per, LRU_TILE, LRU_TILE).astype(bf16)


def _rows(a):
    return a[:, None, :]


def _pad_last(a, width):
    return jnp.pad(a, [(0, 0)] * (a.ndim - 1) + [(0, width - a.shape[-1])])


def kernel(x_prompt, x_sample, mem_prompt, state_lru_h, state_lru_conv, state_gla_S, state_ssd_h, state_ssd_conv, cache_mem_k, cache_mem_v, norm_mix, w_in, lru_conv_w, lru_conv_b, lru_wa, lru_ba, lru_wx, lru_bx, lru_lambda, w_lru_out, gla_w_alpha, gla_b_alpha, gla_norm, w_gla_out, ssd_conv_w, ssd_conv_b, ssd_dt_bias, ssd_a_log, ssd_d, ssd_norm, w_ssd_out, w_mix_out, norm_xattn, norm_mem, w_xq, w_xk, w_xv, w_xo, norm_ffn, w_ffn_gate, w_ffn_up, w_ffn_down, norm_final):
    bp, lp, _ = x_prompt.shape
    bs, ls, _ = x_sample.shape
    assert ls == SUB, "sample kernels treat each 8-row group as one sequence"
    hk = GLA_HEADS * GLA_DK
    xp = x_prompt.reshape(bp * lp, D)
    xs = x_sample.reshape(bs * ls, D)
    mem = mem_prompt.reshape(bp * MEM, D)

    o1 = 2 * D + 2 * hk + 2 * D + GLA_RANK
    o2 = o1 + D + SSD_CH + SSD_HEADS
    zpad = jnp.zeros((DEPTH, D, LANE - GLA_RANK), w_in.dtype)
    w_mix_in = jnp.concatenate([w_in[:, :, :o1], zpad, w_in[:, :, o1:o2], zpad], axis=2).astype(bf16)
    w_gate = w_in[:, :, o2:].astype(bf16)
    g_mix, g_xa, g_mem, g_ffn = _rows(norm_mix), _rows(norm_xattn), _rows(norm_mem), _rows(norm_ffn)
    lru_params = (lru_conv_w, _rows(lru_conv_b), _rows(lru_lambda), _block_diag_tiles(lru_wa), _rows(lru_ba),
                  _block_diag_tiles(lru_wx), _rows(lru_bx))
    gla_params = (jnp.pad(gla_w_alpha, ((0, 0), (0, LANE - GLA_RANK), (0, 0))).astype(bf16),
                  _rows(gla_b_alpha), _rows(gla_norm))
    expand = (jnp.arange(LANE)[:, None] == (jnp.arange(D)[None, :] // SSD_P)).astype(bf16)
    ssd_params = (ssd_conv_w, _rows(ssd_conv_b), _rows(_pad_last(ssd_dt_bias, LANE)),
                  _rows(_pad_last(ssd_a_log, LANE)), _rows(jnp.repeat(ssd_d, SSD_P, axis=1)),
                  _rows(ssd_norm), expand)
    w_lo, w_go, w_so, w_mo = (w.astype(bf16) for w in (w_lru_out, w_gla_out, w_ssd_out, w_mix_out))
    wq, wo = w_xq.astype(bf16), w_xo.astype(bf16)
    wkv = jnp.concatenate([w_xk, w_xv], axis=2).astype(bf16)
    wfg, wfu, wfd = w_ffn_gate.astype(bf16), w_ffn_up.astype(bf16), w_ffn_down.astype(bf16)
    pad5 = ((0, 0), (0, 0), (SUB - (CONV_W - 1), 0), (0, 0))
    bp_lru = jnp.pad(state_lru_conv, pad5).reshape(DEPTH, bs * ls, D)
    bp_ssd = jnp.pad(state_ssd_conv, pad5).reshape(DEPTH, bs * ls, SSD_CH)
    h0_lru = state_lru_h[:, :, None, :]

    p_out = [[] for _ in range(7)]
    s_out = [[] for _ in range(5)]
    for l in range(DEPTH):
        kv = _norm_matmul("mem_kv", mem, g_mem, wkv, l, T_ROW)
        yl, yg, ys, p_h, p_tail, p_s, p_sh, p_stail = _mix_prompt(
            xp, g_mix, w_mix_in, lru_params, gla_params, ssd_params, l, bp, lp, T_MIX)
        xp = _merge("merge_prompt", xp, g_mix, w_gate, yl, yg, ys, w_lo, w_go, w_so, w_mo, l, T_MERGE)
        xp = _xattn_prompt(xp, g_xa, wq, kv.reshape(bp, MEM, 2 * D), wo, l, bp, lp, T_ROW)
        xp = _ffn("ffn_prompt", xp, g_ffn, wfg, wfu, wfd, l, T_ROW)
        p_vals = (p_h[:, 0], p_tail[:, SUB - 3:], p_s, p_sh, p_stail[:, SUB - 3:],
                  kv[:, :D].reshape(bp, MEM, XA_HEADS, XA_HD), kv[:, D:].reshape(bp, MEM, XA_HEADS, XA_HD))
        for lst, val in zip(p_out, p_vals):
            lst.append(val)

        u = _norm_matmul("in_proj_sample", xs, g_mix, w_mix_in, l, T_SAMPLE)
        yl, h_last = _lru_sample(u, h0_lru, bp_lru[l], lru_params, l, T_SAMPLE)
        yg, s_s = _gla_sample(u, state_gla_S, gla_params, l, NB_STATE, ls)
        ys, s_sh = _ssd_sample(u, bp_ssd[l], state_ssd_h, ssd_params, l, NB_STATE, ls)
        xs = _merge("merge_sample", xs, g_mix, w_gate, yl, yg, ys, w_lo, w_go, w_so, w_mo, l, T_SAMPLE)
        xs = _xattn_sample(xs, g_xa, wq, cache_mem_k, cache_mem_v, wo, l, NB_STATE, ls)
        xs = _ffn("ffn_sample", xs, g_ffn, wfg, wfu, wfd, l, T_SAMPLE)
        u3 = u.reshape(bs, ls, W_MIX)
        s_vals = (h_last[:, 0], u3[:, ls - 3:, :D], s_s, s_sh,
                  u3[:, ls - 3:, O_SSD + D:O_SSD + D + SSD_CH])
        for lst, val in zip(s_out, s_vals):
            lst.append(val)

    y_prompt = _final_norm("final_norm_prompt", xp, norm_final.reshape(1, D), T_ROW).reshape(bp, lp, D)
    y_sample = _final_norm("final_norm_sample", xs, norm_final.reshape(1, D), T_SAMPLE).reshape(bs, ls, D)
    p_stack = [jnp.stack(t, axis=0) for t in p_out]
    s_stack = [jnp.stack(t, axis=0) for t in s_out]
    return (y_prompt, y_sample, *p_stack, *s_stack)
```

```python
import functools

import jax
import jax.numpy as jnp
from jax import lax
from jax.experimental import pallas as pl
from jax.experimental.pallas import tpu as pltpu

f32 = jnp.float32
bf16 = jnp.bfloat16

D = 1024
DEPTH = 4
EPS = 1e-6
CONV_W = 4
LRU_C = 8.0
LRU_TILE = 256
GLA_HEADS, GLA_DK, GLA_DV, GLA_RANK, GLA_TAU, GLA_CHUNK = 4, 128, 256, 16, 16.0, 64
SSD_HEADS, SSD_P, SSD_GROUPS, SSD_N, SSD_CHUNK = 16, 64, 2, 64, 128
SSD_HP = SSD_HEADS // SSD_GROUPS * SSD_P
SSD_CH = D + 2 * SSD_GROUPS * SSD_N
XA_HEADS, XA_HD, MEM = 4, 256, 256
D_FF = 2816
LANE = 128
SUB = 8
W_LRU, W_GLA, W_SSD, W_GATE = 2 * D, 3 * D + LANE, D + SSD_CH + LANE, 3 * D
W_MIX = W_LRU + W_GLA + W_SSD
O_GLA, O_SSD = W_LRU, W_LRU + W_GLA
VMEM_LIMIT = 56 * 1024 * 1024

T_MIX = 256
T_ROW = 512
T_FFN = 1024
T_MERGE = 512
T_SAMPLE = 256
NB_STATE = 8


def _cp(n_axes):
    return pltpu.CompilerParams(dimension_semantics=("arbitrary",) * n_axes,
                                vmem_limit_bytes=VMEM_LIMIT)


def _const_spec(shape):
    nd = len(shape)
    return pl.BlockSpec(shape, lambda *_: (0,) * nd, pipeline_mode=pl.Buffered(1))


def _layer_spec(shape, layer):
    nd = len(shape)
    return pl.BlockSpec((None,) + tuple(shape), lambda *_: (layer,) + (0,) * nd,
                        pipeline_mode=pl.Buffered(1))


def _rms(x, g):
    return x * lax.rsqrt(jnp.mean(x * x, axis=-1, keepdims=True) + EPS) * g


def _dot(a, b):
    return jnp.dot(a.astype(bf16), b.astype(bf16), preferred_element_type=f32)


def _dot_nt(a, b):
    return lax.dot_general(a.astype(bf16), b.astype(bf16), (((1,), (1,)), ((), ())),
                           preferred_element_type=f32)


def _split(x):
    hi = x.astype(bf16)
    return hi, (x - hi.astype(f32)).astype(bf16)


def _dot_split(m, x):
    hi, lo = _split(x)
    return (jnp.dot(m, hi, preferred_element_type=f32) + jnp.dot(m, lo, preferred_element_type=f32))


def _dot_split_rows(x, m):
    hi, lo = _split(x)
    return jnp.dot(hi, m, preferred_element_type=f32) + jnp.dot(lo, m, preferred_element_type=f32)


def _bdot(a, b, nb, rows, mode):
    a = a.astype(bf16)
    b = b.astype(bf16)
    if mode == "tn":
        if nb == 1:
            return lax.dot_general(a, b, (((0,), (0,)), ((), ())), preferred_element_type=f32)[None]
        a3 = a.reshape(nb, rows, a.shape[-1])
        b3 = b.reshape(nb, rows, b.shape[-1])
        return jnp.einsum("bki,bkj->bij", a3, b3, preferred_element_type=f32)
    if nb == 1:
        if mode == "nn":
            return jnp.dot(a, b[0], preferred_element_type=f32)
        return lax.dot_general(a, b[0], (((1,), (1,)), ((), ())), preferred_element_type=f32)
    a3 = a.reshape(nb, rows, a.shape[-1])
    eq = "bik,bkj->bij" if mode == "nn" else "bik,bjk->bij"
    out = jnp.einsum(eq, a3, b, preferred_element_type=f32)
    return out.reshape(nb * rows, out.shape[-1])


def _seq_masks(r, c):
    ri = lax.broadcasted_iota(jnp.int32, (r, r), 0)
    ci = lax.broadcasted_iota(jnp.int32, (r, r), 1)
    if r == c:
        same = jnp.full((r, r), True)
    else:
        shift = c.bit_length() - 1
        same = (ri >> shift) == (ci >> shift)
    causal = same & (ri >= ci)
    return causal, same


def _conv_carry(x, tail, w_ref, b_ref):
    xe = jnp.concatenate([tail, x], axis=0)
    y = b_ref[...] + w_ref[0:1, :] * pltpu.roll(xe, 3, 0)[SUB:]
    y = y + w_ref[1:2, :] * pltpu.roll(xe, 2, 0)[SUB:]
    y = y + w_ref[2:3, :] * pltpu.roll(xe, 1, 0)[SUB:]
    return y + w_ref[3:4, :] * x


def _conv_groups(x, bufpad, w_ref, b_ref):
    rows = x.shape[0]
    l = lax.broadcasted_iota(jnp.int32, (rows, 1), 0) & (SUB - 1)
    y = b_ref[...]
    for k in range(CONV_W - 1):
        s = CONV_W - 1 - k
        sh = jnp.where(l >= s, pltpu.roll(x, s, 0), pltpu.roll(bufpad, rows - SUB + s, 0))
        y = y + w_ref[k:k + 1, :] * sh
    return y + w_ref[3:4, :] * x


def _norm_matmul_kernel(x_ref, g_ref, w_ref, o_ref):
    o_ref[...] = _dot(_rms(x_ref[...], g_ref[...]), w_ref[...])


def _norm_matmul(name, x, g, w, layer, tm):
    m, k = x.shape
    n = w.shape[-1]
    return pl.pallas_call(
        _norm_matmul_kernel,
        name=name,
        out_shape=jax.ShapeDtypeStruct((m, n), f32),
        grid=(m // tm,),
        in_specs=[pl.BlockSpec((tm, k), lambda i: (i, 0)), _layer_spec((1, k), layer), _layer_spec((k, n), layer)],
        out_specs=pl.BlockSpec((tm, n), lambda i: (i, 0)),
        compiler_params=_cp(1),
    )(x, g, w)


def _final_norm_kernel(x_ref, g_ref, o_ref):
    o_ref[...] = _rms(x_ref[...], g_ref[...])


def _final_norm(name, x, g, tm):
    m, k = x.shape
    return pl.pallas_call(
        _final_norm_kernel,
        name=name,
        out_shape=jax.ShapeDtypeStruct((m, k), f32),
        grid=(m // tm,),
        in_specs=[pl.BlockSpec((tm, k), lambda i: (i, 0)), _const_spec((1, k))],
        out_specs=pl.BlockSpec((tm, k), lambda i: (i, 0)),
        compiler_params=_cp(1),
    )(x, g)


def _lru_math(xc, lam_ref, wa_ref, ba_ref, wx_ref, bx_ref):
    xcb = xc.astype(bf16)
    nt = D // LRU_TILE
    r_pre = jnp.concatenate([jnp.dot(xcb[:, i * LRU_TILE:(i + 1) * LRU_TILE], wa_ref[i],
                                     preferred_element_type=f32) for i in range(nt)], axis=1)
    i_pre = jnp.concatenate([jnp.dot(xcb[:, i * LRU_TILE:(i + 1) * LRU_TILE], wx_ref[i],
                                     preferred_element_type=f32) for i in range(nt)], axis=1)
    r = jax.nn.sigmoid(r_pre + ba_ref[...])
    ig = jax.nn.sigmoid(i_pre + bx_ref[...])
    log_a = -LRU_C * r * jax.nn.softplus(-lam_ref[...])
    a = jnp.exp(log_a)
    t = jnp.tanh(log_a)
    mult = jnp.sqrt(-2.0 * t / (1.0 - t))
    return a, mult * (ig * xc)


def _group_scan(a, b):
    l = lax.broadcasted_iota(jnp.int32, (a.shape[0], 1), 0) & (SUB - 1)
    for s in (1, 2, 4):
        m = l >= s
        b = jnp.where(m, a * pltpu.roll(b, s, 0) + b, b)
        a = jnp.where(m, a * pltpu.roll(a, s, 0), a)
    return a, b


def _lru_prompt_tiles(hn, w_ref, p, yl_ref, lh_ref, ltail_ref, hc_scr, ltail_scr):
    cw_ref, cb_ref, lam_ref, wa_ref, ba_ref, wx_ref, bx_ref = p
    rows = hn.shape[0]
    for ci in range(D // LRU_TILE):
        cs = slice(ci * LRU_TILE, (ci + 1) * LRU_TILE)
        x = jnp.dot(hn, w_ref[:, cs], preferred_element_type=f32)
        gate = jnp.dot(hn, w_ref[:, D + ci * LRU_TILE:D + (ci + 1) * LRU_TILE], preferred_element_type=f32)
        xe = jnp.concatenate([ltail_scr[:, cs], x], axis=0)
        xc = cb_ref[:, cs] + cw_ref[0:1, cs] * pltpu.roll(xe, 3, 0)[SUB:]
        xc = xc + cw_ref[1:2, cs] * pltpu.roll(xe, 2, 0)[SUB:]
        xc = xc + cw_ref[2:3, cs] * pltpu.roll(xe, 1, 0)[SUB:]
        xc = xc + cw_ref[3:4, cs] * x
        xcb = xc.astype(bf16)
        r = jax.nn.sigmoid(jnp.dot(xcb, wa_ref[ci], preferred_element_type=f32) + ba_ref[:, cs])
        ig = jax.nn.sigmoid(jnp.dot(xcb, wx_ref[ci], preferred_element_type=f32) + bx_ref[:, cs])
        log_a = -LRU_C * r * jax.nn.softplus(-lam_ref[:, cs])
        a = jnp.exp(log_a)
        t = jnp.tanh(log_a)
        b = jnp.sqrt(-2.0 * t / (1.0 - t)) * (ig * xc)
        a, b = _group_scan(a, b)
        carry = hc_scr[0:1, cs]
        hs = []
        for j in range(rows // SUB):
            hj = a[j * SUB:(j + 1) * SUB] * carry + b[j * SUB:(j + 1) * SUB]
            carry = hj[SUB - 1:SUB]
            hs.append(hj)
        yl_ref[:, cs] = jnp.concatenate(hs, axis=0) * jax.nn.gelu(gate)
        hc = jnp.broadcast_to(carry, (SUB, LRU_TILE))
        hc_scr[:, cs] = hc
        lh_ref[0, :, cs] = hc
        ltail = x[rows - SUB:]
        ltail_scr[:, cs] = ltail
        ltail_ref[0, :, cs] = ltail


def _lru_sample_kernel(u_ref, h0_ref, bp_ref, cw_ref, cb_ref, lam_ref, wa_ref, ba_ref, wx_ref, bx_ref,
                       y_ref, h_ref):
    x = u_ref[:, 0:D]
    gate = u_ref[:, D:W_LRU]
    xc = _conv_groups(x, bp_ref[...], cw_ref, cb_ref)
    a, b = _lru_math(xc, lam_ref, wa_ref, ba_ref, wx_ref, bx_ref)
    a, b = _group_scan(a, b)
    nseq = x.shape[0] // SUB
    h3 = a.reshape(nseq, SUB, D) * h0_ref[...] + b.reshape(nseq, SUB, D)
    h_ref[...] = h3[:, SUB - 1:SUB, :]
    y_ref[...] = h3.reshape(nseq * SUB, D) * jax.nn.gelu(gate)


def _lru_param_specs(layer):
    nt = D // LRU_TILE
    return [_layer_spec((CONV_W, D), layer), _layer_spec((1, D), layer), _layer_spec((1, D), layer),
            _layer_spec((nt, LRU_TILE, LRU_TILE), layer), _layer_spec((1, D), layer),
            _layer_spec((nt, LRU_TILE, LRU_TILE), layer), _layer_spec((1, D), layer)]


def _lru_sample(u, h0, bufpad, params, layer, tm):
    m = u.shape[0]
    row = lambda i: (i, 0)
    seq3 = lambda i: (i, 0, 0)
    return pl.pallas_call(
        _lru_sample_kernel,
        name="lru_sample",
        out_shape=(jax.ShapeDtypeStruct((m, D), f32), jax.ShapeDtypeStruct((m // SUB, 1, D), f32)),
        grid=(m // tm,),
        in_specs=[pl.BlockSpec((tm, W_MIX), row),
                  pl.BlockSpec((None, tm // SUB, 1, D), lambda i: (layer, i, 0, 0)),
                  pl.BlockSpec((tm, D), row)] + _lru_param_specs(layer),
        out_specs=(pl.BlockSpec((tm, D), row), pl.BlockSpec((tm // SUB, 1, D), seq3)),
        compiler_params=_cp(1),
    )(u, h0, bufpad, *params)


def _gla_core(u, s_list, nb, c, p, chain=False):
    wal_ref, bal_ref, gn_ref = p
    r_rows = nb * c
    hk = GLA_HEADS * GLA_DK
    q = u[:, 0:hk]
    k = u[:, hk:2 * hk]
    v = u[:, 2 * hk:2 * hk + D]
    rg = u[:, 2 * hk + D:2 * hk + 2 * D]
    glow = u[:, 2 * hk + 2 * D:]
    z = _dot(glow, wal_ref[...]) + bal_ref[...]
    g = jax.nn.log_sigmoid(z) / GLA_TAU
    causal, same = _seq_masks(r_rows, c)
    g_hi, g_lo = _split(g)
    tril = causal.astype(bf16)
    ones_sq = same.astype(bf16)
    bcum = jnp.dot(tril, g_hi, preferred_element_type=f32) + jnp.dot(tril, g_lo, preferred_element_type=f32)
    btot = jnp.dot(ones_sq, g_hi, preferred_element_type=f32) + jnp.dot(ones_sq, g_lo, preferred_element_type=f32)
    qt = (q * GLA_DK ** -0.5) * jnp.exp(bcum)
    kt = k * jnp.exp(-bcum)
    ke = k * jnp.exp(btot - bcum)
    ones_v = jnp.ones((r_rows, GLA_DV), bf16)
    outs, new_s = [], []
    for h in range(GLA_HEADS):
        ks = slice(h * GLA_DK, (h + 1) * GLA_DK)
        vs = slice(h * GLA_DV, (h + 1) * GLA_DV)
        s_h = s_list[h]
        att = jnp.where(causal, _dot_nt(qt[:, ks], kt[:, ks]), 0.0)
        blast = _bdot(g_hi[:, ks], ones_v, nb, c, "tn") + _bdot(g_lo[:, ks], ones_v, nb, c, "tn")
        dec = jnp.exp(blast)
        upd = _bdot(ke[:, ks], v[:, vs], nb, c, "tn")
        if chain:
            cur = s_h[0]
            starts = []
            for j in range(nb):
                starts.append(cur)
                cur = dec[j] * cur + upd[j]
            s_in = jnp.stack(starts, axis=0)
            new_s.append(cur[None])
        else:
            s_in = s_h
            new_s.append(dec * s_h + upd)
        o_h = _dot(att, v[:, vs]) + _bdot(qt[:, ks], s_in, nb, c, "nn")
        o_h = o_h * lax.rsqrt(jnp.mean(o_h * o_h, axis=-1, keepdims=True) + EPS) * gn_ref[...]
        outs.append(o_h)
    o = jnp.concatenate(outs, axis=1)
    return o * jax.nn.silu(rg), new_s


def _gla_sample_kernel(u_ref, s0_ref, wal_ref, bal_ref, gn_ref, o_ref, s_ref, *, nb, c):
    s_list = [s0_ref[:, h] for h in range(GLA_HEADS)]
    o, new_s = _gla_core(u_ref[:, O_GLA:O_GLA + W_GLA], s_list, nb, c, (wal_ref, bal_ref, gn_ref))
    o_ref[...] = o
    for h in range(GLA_HEADS):
        s_ref[:, h] = new_s[h]


def _gla_param_specs(layer):
    hk = GLA_HEADS * GLA_DK
    return [_layer_spec((LANE, hk), layer), _layer_spec((1, hk), layer), _layer_spec((1, GLA_DV), layer)]


def _gla_sample(u, s0_all, params, layer, nb, c):
    nseq = s0_all.shape[1]
    blk = (nb, GLA_HEADS, GLA_DK, GLA_DV)
    return pl.pallas_call(
        functools.partial(_gla_sample_kernel, nb=nb, c=c),
        name="gla_sample",
        out_shape=(jax.ShapeDtypeStruct((nseq * c, D), f32), jax.ShapeDtypeStruct(s0_all.shape[1:], f32)),
        grid=(nseq // nb,),
        in_specs=[pl.BlockSpec((nb * c, W_MIX), lambda i: (i, 0)),
                  pl.BlockSpec((None,) + blk, lambda i: (layer, i, 0, 0, 0))] + _gla_param_specs(layer),
        out_specs=(pl.BlockSpec((nb * c, D), lambda i: (i, 0)), pl.BlockSpec(blk, lambda i: (i, 0, 0, 0))),
        compiler_params=_cp(1),
    )(u, s0_all, *params)


def _ssd_core(sz, xbc, sdt, h_list, nb, c, p):
    dtb_ref, alog_ref, dskip_ref, nrm_ref, exp_ref = p
    r_rows = nb * c
    sx = xbc[:, :D]
    dt = jax.nn.softplus(sdt + dtb_ref[...])
    da = dt * (-jnp.exp(alog_ref[...]))
    causal, same = _seq_masks(r_rows, c)
    cum = _dot_split(causal.astype(bf16), da)
    tot = _dot_split(same.astype(bf16), da)
    ecum = jnp.exp(cum)
    wgt = jnp.exp(tot - cum) * dt
    expd = _dot_split_rows(jnp.concatenate([ecum, wgt, da], axis=0), exp_ref[...])
    ecum_x = expd[0:r_rows]
    wgt_x = expd[r_rows:2 * r_rows]
    da_x = expd[2 * r_rows:]
    cum_t = cum.T
    dt_t = dt.T
    lane = lax.broadcasted_iota(jnp.int32, (r_rows, LANE), 1)
    xw = sx * wgt_x
    da_hi, da_lo = _split(da_x)
    ones_n = jnp.ones((r_rows, SSD_N), bf16)
    ys, new_h = [], []
    for grp in range(SSD_GROUPS):
        bm = xbc[:, D + grp * SSD_N:D + (grp + 1) * SSD_N]
        cm = xbc[:, D + (SSD_GROUPS + grp) * SSD_N:D + (SSD_GROUPS + grp + 1) * SSD_N]
        cb = _dot_nt(cm, bm)
        gs = slice(grp * SSD_HP, (grp + 1) * SSD_HP)
        y_inter = _bdot(cm, h_list[grp], nb, c, "nt") * ecum_x[:, gs]
        pairs = []
        for pr in range(SSD_HP // LANE):
            col = grp * SSD_HP + pr * LANE
            xp = sx[:, col:col + LANE]
            acc = None
            for half in range(2):
                hd = (col // SSD_P) + half
                seg = cum[:, hd:hd + 1] - cum_t[hd:hd + 1, :]
                m = cb * jnp.exp(jnp.where(causal, seg, -jnp.inf)) * dt_t[hd:hd + 1, :]
                xh = jnp.where((lane < SSD_P) if half == 0 else (lane >= SSD_P), xp, 0.0)
                part = _dot(m, xh)
                acc = part if acc is None else acc + part
            pairs.append(acc)
        ys.append(jnp.concatenate(pairs, axis=1) + y_inter)
        dec = jnp.exp(_bdot(da_hi[:, gs], ones_n, nb, c, "tn") + _bdot(da_lo[:, gs], ones_n, nb, c, "tn"))
        new_h.append(dec * h_list[grp] + _bdot(xw[:, gs], bm, nb, c, "tn"))
    y = jnp.concatenate(ys, axis=1) + dskip_ref[...] * sx
    y = y * jax.nn.silu(sz)
    yg = y.reshape(r_rows, SSD_GROUPS, SSD_HP)
    yg = yg * lax.rsqrt(jnp.mean(yg * yg, axis=-1, keepdims=True) + EPS)
    return yg.reshape(r_rows, D) * nrm_ref[...], new_h


def _ssd_sample_kernel(u_ref, bp_ref, h0_ref, cw_ref, cb_ref, dtb_ref, alog_ref, dskip_ref, nrm_ref, exp_ref,
                       y_ref, h_ref, *, nb, c):
    sz = u_ref[:, O_SSD:O_SSD + D]
    raw = u_ref[:, O_SSD + D:O_SSD + D + SSD_CH]
    sdt = u_ref[:, O_SSD + D + SSD_CH:O_SSD + W_SSD]
    xbc = jax.nn.silu(_conv_groups(raw, bp_ref[...], cw_ref, cb_ref))
    hpg = SSD_HEADS // SSD_GROUPS
    h_list = [h0_ref[:, grp * hpg:(grp + 1) * hpg].reshape(nb, SSD_HP, SSD_N) for grp in range(SSD_GROUPS)]
    y, new_h = _ssd_core(sz, xbc, sdt, h_list, nb, c, (dtb_ref, alog_ref, dskip_ref, nrm_ref, exp_ref))
    y_ref[...] = y
    for grp in range(SSD_GROUPS):
        h_ref[:, grp * hpg:(grp + 1) * hpg] = new_h[grp].reshape(nb, hpg, SSD_P, SSD_N)


def _ssd_param_specs(layer):
    return [_layer_spec((CONV_W, SSD_CH), layer), _layer_spec((1, SSD_CH), layer), _layer_spec((1, LANE), layer),
            _layer_spec((1, LANE), layer), _layer_spec((1, D), layer), _layer_spec((1, D), layer),
            _const_spec((LANE, D))]


def _ssd_sample(u, bufpad, h0_all, params, layer, nb, c):
    nseq = h0_all.shape[1]
    blk = (nb, SSD_HEADS, SSD_P, SSD_N)
    row = lambda i: (i, 0)
    return pl.pallas_call(
        functools.partial(_ssd_sample_kernel, nb=nb, c=c),
        name="ssd_sample",
        out_shape=(jax.ShapeDtypeStruct((nseq * c, D), f32), jax.ShapeDtypeStruct(h0_all.shape[1:], f32)),
        grid=(nseq // nb,),
        in_specs=[pl.BlockSpec((nb * c, W_MIX), row), pl.BlockSpec((nb * c, SSD_CH), row),
                  pl.BlockSpec((None,) + blk, lambda i: (layer, i, 0, 0, 0))] + _ssd_param_specs(layer),
        out_specs=(pl.BlockSpec((nb * c, D), row), pl.BlockSpec(blk, lambda i: (i, 0, 0, 0))),
        compiler_params=_cp(1),
    )(u, bufpad, h0_all, *params)


def _mix_prompt_kernel(x_ref, g_ref, w_ref, *refs):
    lru_p = refs[0:7]
    gla_p = refs[7:10]
    ssd_cw, ssd_cb = refs[10:12]
    ssd_p = refs[12:17]
    yl_ref, yg_ref, ys_ref, lh_ref, ltail_ref, gs_ref, sh_ref, stail_ref = refs[17:25]
    hc_scr, ltail_scr, s_scr, h_scr, stail_scr = refs[25:30]
    t = pl.program_id(1)

    @pl.when(t == 0)
    def _():
        hc_scr[...] = jnp.zeros_like(hc_scr)
        ltail_scr[...] = jnp.zeros_like(ltail_scr)
        s_scr[...] = jnp.zeros_like(s_scr)
        h_scr[...] = jnp.zeros_like(h_scr)
        stail_scr[...] = jnp.zeros_like(stail_scr)

    hn = _rms(x_ref[...], g_ref[...]).astype(bf16)
    rows = hn.shape[0]

    _lru_prompt_tiles(hn, w_ref, lru_p, yl_ref, lh_ref, ltail_ref, hc_scr, ltail_scr)

    u = jnp.dot(hn, w_ref[:, O_GLA:O_GLA + W_GLA], preferred_element_type=f32)
    s_list = [s_scr[h][None] for h in range(GLA_HEADS)]
    o, s_list = _gla_core(u, s_list, rows // GLA_CHUNK, GLA_CHUNK, gla_p, chain=True)
    yg_ref[...] = o
    for h in range(GLA_HEADS):
        s_scr[h] = s_list[h][0]

    u = jnp.dot(hn, w_ref[:, O_SSD:O_SSD + W_SSD], preferred_element_type=f32)
    sz = u[:, :D]
    raw = u[:, D:D + SSD_CH]
    sdt = u[:, D + SSD_CH:]
    xbc = jax.nn.silu(_conv_carry(raw, stail_scr[...], ssd_cw, ssd_cb))
    h_list = [h_scr[grp][None] for grp in range(SSD_GROUPS)]
    c = SSD_CHUNK
    for j in range(rows // c):
        rs = slice(j * c, (j + 1) * c)
        y, h_list = _ssd_core(sz[rs], xbc[rs], sdt[rs], h_list, 1, c, ssd_p)
        ys_ref[rs, :] = y
    for grp in range(SSD_GROUPS):
        h_scr[grp] = h_list[grp][0]
    stail = raw[rows - SUB:]
    stail_scr[...] = stail
    stail_ref[0] = stail

    @pl.when(t == pl.num_programs(1) - 1)
    def _():
        gs_ref[0] = s_scr[...]
        hpg = SSD_HEADS // SSD_GROUPS
        for grp in range(SSD_GROUPS):
            sh_ref[0, grp * hpg:(grp + 1) * hpg] = h_scr[grp].reshape(hpg, SSD_P, SSD_N)


def _mix_prompt(x, g, w, lru_params, gla_params, ssd_params, layer, nseq, seqlen, tt):
    nt = seqlen // tt
    row = lambda b, t: (b * nt + t, 0)
    seq3 = lambda b, t: (b, 0, 0)
    seq4 = lambda b, t: (b, 0, 0, 0)
    rows = nseq * seqlen
    return pl.pallas_call(
        _mix_prompt_kernel,
        name="mix_prompt",
        out_shape=(jax.ShapeDtypeStruct((rows, D), f32), jax.ShapeDtypeStruct((rows, D), f32),
                   jax.ShapeDtypeStruct((rows, D), f32),
                   jax.ShapeDtypeStruct((nseq, SUB, D), f32), jax.ShapeDtypeStruct((nseq, SUB, D), f32),
                   jax.ShapeDtypeStruct((nseq, GLA_HEADS, GLA_DK, GLA_DV), f32),
                   jax.ShapeDtypeStruct((nseq, SSD_HEADS, SSD_P, SSD_N), f32),
                   jax.ShapeDtypeStruct((nseq, SUB, SSD_CH), f32)),
        grid=(nseq, nt),
        in_specs=[pl.BlockSpec((tt, D), row), _layer_spec((1, D), layer), _layer_spec((D, W_MIX), layer)]
        + _lru_param_specs(layer) + _gla_param_specs(layer) + _ssd_param_specs(layer),
        out_specs=(pl.BlockSpec((tt, D), row), pl.BlockSpec((tt, D), row), pl.BlockSpec((tt, D), row),
                   pl.BlockSpec((1, SUB, D), seq3), pl.BlockSpec((1, SUB, D), seq3),
                   pl.BlockSpec((1, GLA_HEADS, GLA_DK, GLA_DV), seq4),
                   pl.BlockSpec((1, SSD_HEADS, SSD_P, SSD_N), seq4),
                   pl.BlockSpec((1, SUB, SSD_CH), seq3)),
        scratch_shapes=[pltpu.VMEM((SUB, D), f32), pltpu.VMEM((SUB, D), f32),
                        pltpu.VMEM((GLA_HEADS, GLA_DK, GLA_DV), f32),
                        pltpu.VMEM((SSD_GROUPS, SSD_HP, SSD_N), f32), pltpu.VMEM((SUB, SSD_CH), f32)],
        compiler_params=_cp(2),
    )(x, g, w, *lru_params, *gla_params, *ssd_params)


def _merge_kernel(x_ref, g_ref, wg_ref, yl_ref, yg_ref, ys_ref, wl_ref, wgl_ref, ws_ref, wm_ref, o_ref):
    x = x_ref[...]
    gates = jax.nn.sigmoid(_dot(_rms(x, g_ref[...]), wg_ref[...]))
    merged = gates[:, 0:D] * _dot(yl_ref[...], wl_ref[...])
    merged = merged + gates[:, D:2 * D] * _dot(yg_ref[...], wgl_ref[...])
    merged = merged + gates[:, 2 * D:] * _dot(ys_ref[...], ws_ref[...])
    o_ref[...] = x + _dot(merged, wm_ref[...])


def _merge(name, x, g, wgate, yl, yg, ys, wl, wgl, ws, wm, layer, tm):
    m = x.shape[0]
    row = pl.BlockSpec((tm, D), lambda i: (i, 0))
    wsq = _layer_spec((D, D), layer)
    return pl.pallas_call(
        _merge_kernel,
        name=name,
        out_shape=jax.ShapeDtypeStruct((m, D), f32),
        grid=(m // tm,),
        in_specs=[row, _layer_spec((1, D), layer), _layer_spec((D, W_GATE), layer), row, row, row,
                  wsq, wsq, wsq, wsq],
        out_specs=row,
        compiler_params=_cp(1),
    )(x, g, wgate, yl, yg, ys, wl, wgl, ws, wm)


def _attend(q, get_k, get_v, nb, rows):
    outs = []
    for h in range(XA_HEADS):
        hs = slice(h * XA_HD, (h + 1) * XA_HD)
        s = _bdot(q[:, hs], get_k(h), nb, rows, "nt") * XA_HD ** -0.5
        s = s - jnp.max(s, axis=-1, keepdims=True)
        e = jnp.exp(s)
        p = e / jnp.sum(e, axis=-1, keepdims=True)
        outs.append(_bdot(p, get_v(h), nb, rows, "nn"))
    return jnp.concatenate(outs, axis=1)


def _xattn_prompt_kernel(x_ref, g_ref, wq_ref, k_ref, v_ref, wo_ref, o_ref):
    x = x_ref[...]
    q = _dot(_rms(x, g_ref[...]), wq_ref[...])
    o = _attend(q, lambda h: k_ref[:, :, h * XA_HD:(h + 1) * XA_HD],
                lambda h: v_ref[:, :, h * XA_HD:(h + 1) * XA_HD], 1, x.shape[0])
    o_ref[...] = x + _dot(o, wo_ref[...])


def _xattn_sample_kernel(x_ref, g_ref, wq_ref, k_ref, v_ref, wo_ref, o_ref, *, nb, rows):
    x = x_ref[...]
    q = _dot(_rms(x, g_ref[...]), wq_ref[...]).reshape(nb, rows, D)
    qh = jnp.concatenate([q[:, :, h * XA_HD:(h + 1) * XA_HD] for h in range(XA_HEADS)], axis=1)
    k2 = k_ref[...].reshape(nb, MEM * XA_HEADS, XA_HD).astype(bf16)
    v2 = v_ref[...].reshape(nb, MEM * XA_HEADS, XA_HD).astype(bf16)
    s = jnp.einsum("bik,bjk->bij", qh.astype(bf16), k2, preferred_element_type=f32) * XA_HD ** -0.5
    shp = (XA_HEADS * rows, MEM * XA_HEADS)
    q_head = lax.broadcasted_iota(jnp.int32, shp, 0) // rows
    m_head = lax.broadcasted_iota(jnp.int32, shp, 1) & (XA_HEADS - 1)
    s = jnp.where((q_head == m_head)[None], s, -jnp.inf)
    s = s - jnp.max(s, axis=-1, keepdims=True)
    e = jnp.exp(s)
    p = e / jnp.sum(e, axis=-1, keepdims=True)
    o2 = jnp.einsum("bij,bjk->bik", p.astype(bf16), v2, preferred_element_type=f32)
    o = jnp.concatenate([o2[:, h * rows:(h + 1) * rows, :] for h in range(XA_HEADS)], axis=2)
    o_ref[...] = x + _dot(o.reshape(nb * rows, D), wo_ref[...])


def _xattn_prompt(x, g, wq, kv, wo, layer, nseq, seqlen, tq):
    nt = seqlen // tq
    row = pl.BlockSpec((tq, D), lambda b, t: (b * nt + t, 0))
    return pl.pallas_call(
        _xattn_prompt_kernel,
        name="xattn_prompt",
        out_shape=jax.ShapeDtypeStruct(x.shape, f32),
        grid=(nseq, nt),
        in_specs=[row, _layer_spec((1, D), layer), _layer_spec((D, D), layer),
                  pl.BlockSpec((1, MEM, D), lambda b, t: (b, 0, 0)),
                  pl.BlockSpec((1, MEM, D), lambda b, t: (b, 0, 1)), _layer_spec((D, D), layer)],
        out_specs=row,
        compiler_params=_cp(2),
    )(x, g, wq, kv, kv, wo)


def _xattn_sample(x, g, wq, k_all, v_all, wo, layer, nb, rows):
    nseq = k_all.shape[1]
    row = pl.BlockSpec((nb * rows, D), lambda i: (i, 0))
    kvspec = pl.BlockSpec((None, nb, MEM, XA_HEADS, XA_HD), lambda i: (layer, i, 0, 0, 0))
    return pl.pallas_call(
        functools.partial(_xattn_sample_kernel, nb=nb, rows=rows),
        name="xattn_sample",
        out_shape=jax.ShapeDtypeStruct(x.shape, f32),
        grid=(nseq // nb,),
        in_specs=[row, _layer_spec((1, D), layer), _layer_spec((D, D), layer), kvspec, kvspec,
                  _layer_spec((D, D), layer)],
        out_specs=row,
        compiler_params=_cp(1),
    )(x, g, wq, k_all, v_all, wo)


FF_SPLIT = 2


def _ffn_kernel(x_ref, g_ref, wg_ref, wu_ref, wd_ref, o_ref):
    x = x_ref[...]
    h = _rms(x, g_ref[...]).astype(bf16)
    fc = D_FF // FF_SPLIT
    acc = x
    for j in range(FF_SPLIT):
        cs = slice(j * fc, (j + 1) * fc)
        gate = jnp.dot(h, wg_ref[:, cs], preferred_element_type=f32)
        up = jnp.dot(h, wu_ref[:, cs], preferred_element_type=f32)
        acc = acc + _dot(jax.nn.silu(gate) * up, wd_ref[cs, :])
    o_ref[...] = acc


def _ffn(name, x, g, wg, wu, wd, layer, tm):
    m = x.shape[0]
    row = pl.BlockSpec((tm, D), lambda i: (i, 0))
    return pl.pallas_call(
        _ffn_kernel,
        name=name,
        out_shape=jax.ShapeDtypeStruct((m, D), f32),
        grid=(m // tm,),
        in_specs=[row, _layer_spec((1, D), layer), _layer_spec((D, D_FF), layer), _layer_spec((D, D_FF), layer),
                  _layer_spec((D_FF, D), layer)],
        out_specs=row,
        compiler_params=_cp(1),
    )(x, g, wg, wu, wd)


def _block_diag_tiles(w):
    depth, nblk, bs, _ = w.shape
    per = LRU_TILE // bs
    w4 = w.reshape(depth, nblk // per, per, bs, bs)
    eye = jnp.eye(per, dtype=w.dtype)
    t = jnp.einsum("ltpij,pq->ltpiqj", w4, eye)
    return t.reshape(depth, nblk // per, LRU_TILE, LRU_TILE).astype(bf16)


def _rows(a):
    return a[:, None, :]


def _pad_last(a, width):
    return jnp.pad(a, [(0, 0)] * (a.ndim - 1) + [(0, width - a.shape[-1])])


def kernel(x_prompt, x_sample, mem_prompt, state_lru_h, state_lru_conv, state_gla_S, state_ssd_h, state_ssd_conv, cache_mem_k, cache_mem_v, norm_mix, w_in, lru_conv_w, lru_conv_b, lru_wa, lru_ba, lru_wx, lru_bx, lru_lambda, w_lru_out, gla_w_alpha, gla_b_alpha, gla_norm, w_gla_out, ssd_conv_w, ssd_conv_b, ssd_dt_bias, ssd_a_log, ssd_d, ssd_norm, w_ssd_out, w_mix_out, norm_xattn, norm_mem, w_xq, w_xk, w_xv, w_xo, norm_ffn, w_ffn_gate, w_ffn_up, w_ffn_down, norm_final):
    bp, lp, _ = x_prompt.shape
    bs, ls, _ = x_sample.shape
    assert ls == SUB, "sample kernels treat each 8-row group as one sequence"
    hk = GLA_HEADS * GLA_DK
    xp = x_prompt.reshape(bp * lp, D)
    xs = x_sample.reshape(bs * ls, D)
    mem = mem_prompt.reshape(bp * MEM, D)

    o1 = 2 * D + 2 * hk + 2 * D + GLA_RANK
    o2 = o1 + D + SSD_CH + SSD_HEADS
    w_in16 = w_in.astype(bf16)
    zpad = jnp.zeros((DEPTH, D, LANE - GLA_RANK), bf16)
    w_mix_in = jnp.concatenate([w_in16[:, :, :o1], zpad, w_in16[:, :, o1:o2], zpad], axis=2)
    w_gate = w_in16[:, :, o2:]
    g_mix, g_xa, g_mem, g_ffn = _rows(norm_mix), _rows(norm_xattn), _rows(norm_mem), _rows(norm_ffn)
    lru_params = (lru_conv_w, _rows(lru_conv_b), _rows(lru_lambda), _block_diag_tiles(lru_wa), _rows(lru_ba),
                  _block_diag_tiles(lru_wx), _rows(lru_bx))
    gla_params = (jnp.pad(gla_w_alpha, ((0, 0), (0, LANE - GLA_RANK), (0, 0))).astype(bf16),
                  _rows(gla_b_alpha), _rows(gla_norm))
    expand = (jnp.arange(LANE)[:, None] == (jnp.arange(D)[None, :] // SSD_P)).astype(bf16)
    ssd_params = (ssd_conv_w, _rows(ssd_conv_b), _rows(_pad_last(ssd_dt_bias, LANE)),
                  _rows(_pad_last(ssd_a_log, LANE)), _rows(jnp.repeat(ssd_d, SSD_P, axis=1)),
                  _rows(ssd_norm), expand)
    w_lo, w_go, w_so, w_mo = (w.astype(bf16) for w in (w_lru_out, w_gla_out, w_ssd_out, w_mix_out))
    wq, wo = w_xq.astype(bf16), w_xo.astype(bf16)
    wkv = jnp.concatenate([w_xk, w_xv], axis=2).astype(bf16)
    wfg, wfu, wfd = w_ffn_gate.astype(bf16), w_ffn_up.astype(bf16), w_ffn_down.astype(bf16)
    pad5 = ((0, 0), (0, 0), (SUB - (CONV_W - 1), 0), (0, 0))
    bp_lru = jnp.pad(state_lru_conv, pad5).reshape(DEPTH, bs * ls, D)
    bp_ssd = jnp.pad(state_ssd_conv, pad5).reshape(DEPTH, bs * ls, SSD_CH)
    h0_lru = state_lru_h[:, :, None, :]

    p_out = [[] for _ in range(7)]
    s_out = [[] for _ in range(5)]
    for l in range(DEPTH):
        kv = _norm_matmul("mem_kv", mem, g_mem, wkv, l, T_ROW)
        yl, yg, ys, p_h, p_tail, p_s, p_sh, p_stail = _mix_prompt(
            xp, g_mix, w_mix_in, lru_params, gla_params, ssd_params, l, bp, lp, T_MIX)
        xp = _merge("merge_prompt", xp, g_mix, w_gate, yl, yg, ys, w_lo, w_go, w_so, w_mo, l, T_MERGE)
        xp = _xattn_prompt(xp, g_xa, wq, kv.reshape(bp, MEM, 2 * D), wo, l, bp, lp, T_ROW)
        xp = _ffn("ffn_prompt", xp, g_ffn, wfg, wfu, wfd, l, T_FFN)
        p_vals = (p_h[:, 0], p_tail[:, SUB - 3:], p_s, p_sh, p_stail[:, SUB - 3:],
                  kv[:, :D].reshape(bp, MEM, XA_HEADS, XA_HD), kv[:, D:].reshape(bp, MEM, XA_HEADS, XA_HD))
        for lst, val in zip(p_out, p_vals):
            lst.append(val)

        u = _norm_matmul("in_proj_sample", xs, g_mix, w_mix_in, l, T_SAMPLE)
        yl, h_last = _lru_sample(u, h0_lru, bp_lru[l], lru_params, l, T_SAMPLE)
        yg, s_s = _gla_sample(u, state_gla_S, gla_params, l, NB_STATE, ls)
        ys, s_sh = _ssd_sample(u, bp_ssd[l], state_ssd_h, ssd_params, l, NB_STATE, ls)
        xs = _merge("merge_sample", xs, g_mix, w_gate, yl, yg, ys, w_lo, w_go, w_so, w_mo, l, T_SAMPLE)
        xs = _xattn_sample(xs, g_xa, wq, cache_mem_k, cache_mem_v, wo, l, NB_STATE, ls)
        xs = _ffn("ffn_sample", xs, g_ffn, wfg, wfu, wfd, l, T_SAMPLE)
        u3 = u.reshape(bs, ls, W_MIX)
        s_vals = (h_last[:, 0], u3[:, ls - 3:, :D], s_s, s_sh,
                  u3[:, ls - 3:, O_SSD + D:O_SSD + D + SSD_CH])
        for lst, val in zip(s_out, s_vals):
            lst.append(val)

    y_prompt = _final_norm("final_norm_prompt", xp, norm_final.reshape(1, D), T_ROW).reshape(bp, lp, D)
    y_sample = _final_norm("final_norm_sample", xs, norm_final.reshape(1, D), T_SAMPLE).reshape(bs, ls, D)
    p_stack = [jnp.stack(t, axis=0) for t in p_out]
    s_stack = [jnp.stack(t, axis=0) for t in s_out]
    return (y_prompt, y_sample, *p_stack, *s_stack)
```

```python
import functools

import jax
import jax.numpy as jnp
from jax import lax
from jax.experimental import pallas as pl
from jax.experimental.pallas import tpu as pltpu

f32 = jnp.float32
bf16 = jnp.bfloat16

D = 1024
DEPTH = 4
EPS = 1e-6
CONV_W = 4
LRU_C = 8.0
LRU_TILE = 256
GLA_HEADS, GLA_DK, GLA_DV, GLA_RANK, GLA_TAU, GLA_CHUNK = 4, 128, 256, 16, 16.0, 64
SSD_HEADS, SSD_P, SSD_GROUPS, SSD_N, SSD_CHUNK = 16, 64, 2, 64, 128
SSD_HP = SSD_HEADS // SSD_GROUPS * SSD_P
SSD_CH = D + 2 * SSD_GROUPS * SSD_N
XA_HEADS, XA_HD, MEM = 4, 256, 256
D_FF = 2816
LANE = 128
SUB = 8
W_LRU, W_GLA, W_SSD, W_GATE = 2 * D, 3 * D + LANE, D + SSD_CH + LANE, 3 * D
W_MIX = W_LRU + W_GLA + W_SSD
O_GLA, O_SSD = W_LRU, W_LRU + W_GLA
VMEM_LIMIT = 56 * 1024 * 1024

T_MIX = 256
T_ROW = 512
T_FFN = 1024
T_MERGE = 512
T_SAMPLE = 256
NB_STATE = 8


def _cp(n_axes):
    return pltpu.CompilerParams(dimension_semantics=("arbitrary",) * n_axes,
                                vmem_limit_bytes=VMEM_LIMIT)


def _const_spec(shape):
    nd = len(shape)
    return pl.BlockSpec(shape, lambda *_: (0,) * nd, pipeline_mode=pl.Buffered(1))


def _layer_spec(shape, layer):
    nd = len(shape)
    return pl.BlockSpec((None,) + tuple(shape), lambda *_: (layer,) + (0,) * nd,
                        pipeline_mode=pl.Buffered(1))


def _rms(x, g):
    return x * lax.rsqrt(jnp.mean(x * x, axis=-1, keepdims=True) + EPS) * g


def _dot(a, b):
    return jnp.dot(a.astype(bf16), b.astype(bf16), preferred_element_type=f32)


def _dot_nt(a, b):
    return lax.dot_general(a.astype(bf16), b.astype(bf16), (((1,), (1,)), ((), ())),
                           preferred_element_type=f32)


def _split(x):
    hi = x.astype(bf16)
    return hi, (x - hi.astype(f32)).astype(bf16)


def _dot_split(m, x):
    hi, lo = _split(x)
    return (jnp.dot(m, hi, preferred_element_type=f32) + jnp.dot(m, lo, preferred_element_type=f32))


def _dot_split_rows(x, m):
    hi, lo = _split(x)
    return jnp.dot(hi, m, preferred_element_type=f32) + jnp.dot(lo, m, preferred_element_type=f32)


def _bdot(a, b, nb, rows, mode):
    a = a.astype(bf16)
    b = b.astype(bf16)
    if mode == "tn":
        if nb == 1:
            return lax.dot_general(a, b, (((0,), (0,)), ((), ())), preferred_element_type=f32)[None]
        a3 = a.reshape(nb, rows, a.shape[-1])
        b3 = b.reshape(nb, rows, b.shape[-1])
        return jnp.einsum("bki,bkj->bij", a3, b3, preferred_element_type=f32)
    if nb == 1:
        if mode == "nn":
            return jnp.dot(a, b[0], preferred_element_type=f32)
        return lax.dot_general(a, b[0], (((1,), (1,)), ((), ())), preferred_element_type=f32)
    a3 = a.reshape(nb, rows, a.shape[-1])
    eq = "bik,bkj->bij" if mode == "nn" else "bik,bjk->bij"
    out = jnp.einsum(eq, a3, b, preferred_element_type=f32)
    return out.reshape(nb * rows, out.shape[-1])


def _seq_masks(r, c):
    ri = lax.broadcasted_iota(jnp.int32, (r, r), 0)
    ci = lax.broadcasted_iota(jnp.int32, (r, r), 1)
    if r == c:
        same = jnp.full((r, r), True)
    else:
        shift = c.bit_length() - 1
        same = (ri >> shift) == (ci >> shift)
    causal = same & (ri >= ci)
    return causal, same


def _conv_carry(x, tail, w_ref, b_ref):
    xe = jnp.concatenate([tail, x], axis=0)
    y = b_ref[...] + w_ref[0:1, :] * pltpu.roll(xe, 3, 0)[SUB:]
    y = y + w_ref[1:2, :] * pltpu.roll(xe, 2, 0)[SUB:]
    y = y + w_ref[2:3, :] * pltpu.roll(xe, 1, 0)[SUB:]
    return y + w_ref[3:4, :] * x


def _conv_groups(x, bufpad, w_ref, b_ref):
    rows = x.shape[0]
    l = lax.broadcasted_iota(jnp.int32, (rows, 1), 0) & (SUB - 1)
    y = b_ref[...]
    for k in range(CONV_W - 1):
        s = CONV_W - 1 - k
        sh = jnp.where(l >= s, pltpu.roll(x, s, 0), pltpu.roll(bufpad, rows - SUB + s, 0))
        y = y + w_ref[k:k + 1, :] * sh
    return y + w_ref[3:4, :] * x


def _norm_matmul_kernel(x_ref, g_ref, w_ref, o_ref):
    o_ref[...] = _dot(_rms(x_ref[...], g_ref[...]), w_ref[...])


def _norm_matmul(name, x, g, w, layer, tm):
    m, k = x.shape
    n = w.shape[-1]
    return pl.pallas_call(
        _norm_matmul_kernel,
        name=name,
        out_shape=jax.ShapeDtypeStruct((m, n), f32),
        grid=(m // tm,),
        in_specs=[pl.BlockSpec((tm, k), lambda i: (i, 0)), _layer_spec((1, k), layer), _layer_spec((k, n), layer)],
        out_specs=pl.BlockSpec((tm, n), lambda i: (i, 0)),
        compiler_params=_cp(1),
    )(x, g, w)


def _final_norm_kernel(x_ref, g_ref, o_ref):
    o_ref[...] = _rms(x_ref[...], g_ref[...])


def _final_norm(name, x, g, tm):
    m, k = x.shape
    return pl.pallas_call(
        _final_norm_kernel,
        name=name,
        out_shape=jax.ShapeDtypeStruct((m, k), f32),
        grid=(m // tm,),
        in_specs=[pl.BlockSpec((tm, k), lambda i: (i, 0)), _const_spec((1, k))],
        out_specs=pl.BlockSpec((tm, k), lambda i: (i, 0)),
        compiler_params=_cp(1),
    )(x, g)


def _lru_decay_and_scale(log_a):
    a = jnp.exp(log_a)
    m = 1.0 - a * a
    return a, jnp.where(m > 0.0, m * lax.rsqrt(m), 0.0)


def _lru_math(xc, lam_ref, wa_ref, ba_ref, wx_ref, bx_ref):
    xcb = xc.astype(bf16)
    nt = D // LRU_TILE
    r_pre = jnp.concatenate([jnp.dot(xcb[:, i * LRU_TILE:(i + 1) * LRU_TILE], wa_ref[i],
                                     preferred_element_type=f32) for i in range(nt)], axis=1)
    i_pre = jnp.concatenate([jnp.dot(xcb[:, i * LRU_TILE:(i + 1) * LRU_TILE], wx_ref[i],
                                     preferred_element_type=f32) for i in range(nt)], axis=1)
    r = jax.nn.sigmoid(r_pre + ba_ref[...])
    ig = jax.nn.sigmoid(i_pre + bx_ref[...])
    a, mult = _lru_decay_and_scale(-LRU_C * r * jax.nn.softplus(-lam_ref[...]))
    return a, mult * (ig * xc)


def _group_scan(a, b):
    l = lax.broadcasted_iota(jnp.int32, (a.shape[0], 1), 0) & (SUB - 1)
    for s in (1, 2, 4):
        m = l >= s
        b = jnp.where(m, a * pltpu.roll(b, s, 0) + b, b)
        a = jnp.where(m, a * pltpu.roll(a, s, 0), a)
    return a, b


def _lru_prompt_tiles(hn, w_ref, p, yl_ref, lh_ref, ltail_ref, hc_scr, ltail_scr):
    cw_ref, cb_ref, lam_ref, wa_ref, ba_ref, wx_ref, bx_ref = p
    rows = hn.shape[0]
    for ci in range(D // LRU_TILE):
        cs = slice(ci * LRU_TILE, (ci + 1) * LRU_TILE)
        x = jnp.dot(hn, w_ref[:, cs], preferred_element_type=f32)
        gate = jnp.dot(hn, w_ref[:, D + ci * LRU_TILE:D + (ci + 1) * LRU_TILE], preferred_element_type=f32)
        xe = jnp.concatenate([ltail_scr[:, cs], x], axis=0)
        xc = cb_ref[:, cs] + cw_ref[0:1, cs] * pltpu.roll(xe, 3, 0)[SUB:]
        xc = xc + cw_ref[1:2, cs] * pltpu.roll(xe, 2, 0)[SUB:]
        xc = xc + cw_ref[2:3, cs] * pltpu.roll(xe, 1, 0)[SUB:]
        xc = xc + cw_ref[3:4, cs] * x
        xcb = xc.astype(bf16)
        r = jax.nn.sigmoid(jnp.dot(xcb, wa_ref[ci], preferred_element_type=f32) + ba_ref[:, cs])
        ig = jax.nn.sigmoid(jnp.dot(xcb, wx_ref[ci], preferred_element_type=f32) + bx_ref[:, cs])
        a, mult = _lru_decay_and_scale(-LRU_C * r * jax.nn.softplus(-lam_ref[:, cs]))
        b = mult * (ig * xc)
        a, b = _group_scan(a, b)
        carry = hc_scr[0:1, cs]
        hs = []
        for j in range(rows // SUB):
            hj = a[j * SUB:(j + 1) * SUB] * carry + b[j * SUB:(j + 1) * SUB]
            carry = hj[SUB - 1:SUB]
            hs.append(hj)
        yl_ref[:, cs] = jnp.concatenate(hs, axis=0) * jax.nn.gelu(gate)
        hc = jnp.broadcast_to(carry, (SUB, LRU_TILE))
        hc_scr[:, cs] = hc
        lh_ref[0, :, cs] = hc
        ltail = x[rows - SUB:]
        ltail_scr[:, cs] = ltail
        ltail_ref[0, :, cs] = ltail


def _lru_sample_kernel(u_ref, h0_ref, bp_ref, cw_ref, cb_ref, lam_ref, wa_ref, ba_ref, wx_ref, bx_ref,
                       y_ref, h_ref):
    x = u_ref[:, 0:D]
    gate = u_ref[:, D:W_LRU]
    xc = _conv_groups(x, bp_ref[...], cw_ref, cb_ref)
    a, b = _lru_math(xc, lam_ref, wa_ref, ba_ref, wx_ref, bx_ref)
    a, b = _group_scan(a, b)
    nseq = x.shape[0] // SUB
    h3 = a.reshape(nseq, SUB, D) * h0_ref[...] + b.reshape(nseq, SUB, D)
    h_ref[...] = h3[:, SUB - 1:SUB, :]
    y_ref[...] = h3.reshape(nseq * SUB, D) * jax.nn.gelu(gate)


def _lru_param_specs(layer):
    nt = D // LRU_TILE
    return [_layer_spec((CONV_W, D), layer), _layer_spec((1, D), layer), _layer_spec((1, D), layer),
            _layer_spec((nt, LRU_TILE, LRU_TILE), layer), _layer_spec((1, D), layer),
            _layer_spec((nt, LRU_TILE, LRU_TILE), layer), _layer_spec((1, D), layer)]


def _lru_sample(u, h0, bufpad, params, layer, tm):
    m = u.shape[0]
    row = lambda i: (i, 0)
    seq3 = lambda i: (i, 0, 0)
    return pl.pallas_call(
        _lru_sample_kernel,
        name="lru_sample",
        out_shape=(jax.ShapeDtypeStruct((m, D), f32), jax.ShapeDtypeStruct((m // SUB, 1, D), f32)),
        grid=(m // tm,),
        in_specs=[pl.BlockSpec((tm, W_MIX), row),
                  pl.BlockSpec((None, tm // SUB, 1, D), lambda i: (layer, i, 0, 0)),
                  pl.BlockSpec((tm, D), row)] + _lru_param_specs(layer),
        out_specs=(pl.BlockSpec((tm, D), row), pl.BlockSpec((tm // SUB, 1, D), seq3)),
        compiler_params=_cp(1),
    )(u, h0, bufpad, *params)


def _gla_core(u, s_list, nb, c, p, chain=False):
    wal_ref, bal_ref, gn_ref = p
    r_rows = nb * c
    hk = GLA_HEADS * GLA_DK
    q = u[:, 0:hk]
    k = u[:, hk:2 * hk]
    v = u[:, 2 * hk:2 * hk + D]
    rg = u[:, 2 * hk + D:2 * hk + 2 * D]
    glow = u[:, 2 * hk + 2 * D:]
    z = _dot(glow, wal_ref[...]) + bal_ref[...]
    g = jax.nn.log_sigmoid(z) / GLA_TAU
    causal, same = _seq_masks(r_rows, c)
    g_hi, g_lo = _split(g)
    tril = causal.astype(bf16)
    ones_sq = same.astype(bf16)
    bcum = jnp.dot(tril, g_hi, preferred_element_type=f32) + jnp.dot(tril, g_lo, preferred_element_type=f32)
    btot = jnp.dot(ones_sq, g_hi, preferred_element_type=f32) + jnp.dot(ones_sq, g_lo, preferred_element_type=f32)
    qt = (q * GLA_DK ** -0.5) * jnp.exp(bcum)
    kt = k * jnp.exp(-bcum)
    ke = k * jnp.exp(btot - bcum)
    ones_v = jnp.ones((r_rows, GLA_DV), bf16)
    if chain:
        assert nb <= SUB
        tot8 = jnp.concatenate([btot[j * c:j * c + 1, :] for j in range(nb)] + [btot[0:SUB - nb, :]], axis=0)
        dec_t = jnp.exp(tot8.T)
    outs, new_s = [], []
    for h in range(GLA_HEADS):
        ks = slice(h * GLA_DK, (h + 1) * GLA_DK)
        vs = slice(h * GLA_DV, (h + 1) * GLA_DV)
        s_h = s_list[h]
        att = jnp.where(causal, _dot_nt(qt[:, ks], kt[:, ks]), 0.0)
        upd = _bdot(ke[:, ks], v[:, vs], nb, c, "tn")
        if chain:
            dec = [dec_t[h * GLA_DK:(h + 1) * GLA_DK, j:j + 1] for j in range(nb)]
        else:
            blast = _bdot(g_hi[:, ks], ones_v, nb, c, "tn") + _bdot(g_lo[:, ks], ones_v, nb, c, "tn")
            dec = jnp.exp(blast)
        if chain:
            cur = s_h[0]
            starts = []
            for j in range(nb):
                starts.append(cur)
                cur = dec[j] * cur + upd[j]
            s_in = jnp.stack(starts, axis=0)
            new_s.append(cur[None])
        else:
            s_in = s_h
            new_s.append(dec * s_h + upd)
        o_h = _dot(att, v[:, vs]) + _bdot(qt[:, ks], s_in, nb, c, "nn")
        o_h = o_h * lax.rsqrt(jnp.mean(o_h * o_h, axis=-1, keepdims=True) + EPS) * gn_ref[...]
        outs.append(o_h)
    o = jnp.concatenate(outs, axis=1)
    return o * jax.nn.silu(rg), new_s


def _gla_sample_kernel(u_ref, s0_ref, wal_ref, bal_ref, gn_ref, o_ref, s_ref, *, nb, c):
    s_list = [s0_ref[:, h] for h in range(GLA_HEADS)]
    o, new_s = _gla_core(u_ref[:, O_GLA:O_GLA + W_GLA], s_list, nb, c, (wal_ref, bal_ref, gn_ref))
    o_ref[...] = o
    for h in range(GLA_HEADS):
        s_ref[:, h] = new_s[h]


def _gla_param_specs(layer):
    hk = GLA_HEADS * GLA_DK
    return [_layer_spec((LANE, hk), layer), _layer_spec((1, hk), layer), _layer_spec((1, GLA_DV), layer)]


def _gla_sample(u, s0_all, params, layer, nb, c):
    nseq = s0_all.shape[1]
    blk = (nb, GLA_HEADS, GLA_DK, GLA_DV)
    return pl.pallas_call(
        functools.partial(_gla_sample_kernel, nb=nb, c=c),
        name="gla_sample",
        out_shape=(jax.ShapeDtypeStruct((nseq * c, D), f32), jax.ShapeDtypeStruct(s0_all.shape[1:], f32)),
        grid=(nseq // nb,),
        in_specs=[pl.BlockSpec((nb * c, W_MIX), lambda i: (i, 0)),
                  pl.BlockSpec((None,) + blk, lambda i: (layer, i, 0, 0, 0))] + _gla_param_specs(layer),
        out_specs=(pl.BlockSpec((nb * c, D), lambda i: (i, 0)), pl.BlockSpec(blk, lambda i: (i, 0, 0, 0))),
        compiler_params=_cp(1),
    )(u, s0_all, *params)


def _ssd_core(sz, xbc, sdt, h_list, nb, c, p):
    dtb_ref, alog_ref, dskip_ref, nrm_ref, exp_ref = p
    r_rows = nb * c
    sx = xbc[:, :D]
    dt = jax.nn.softplus(sdt + dtb_ref[...])
    da = dt * (-jnp.exp(alog_ref[...]))
    causal, same = _seq_masks(r_rows, c)
    cum = _dot_split(causal.astype(bf16), da)
    tot = _dot_split(same.astype(bf16), da)
    ecum = jnp.exp(cum)
    wgt = jnp.exp(tot - cum) * dt
    expd = _dot_split_rows(jnp.concatenate([ecum, wgt, da], axis=0), exp_ref[...])
    ecum_x = expd[0:r_rows]
    wgt_x = expd[r_rows:2 * r_rows]
    da_x = expd[2 * r_rows:]
    cum_t = cum.T
    dt_t = dt.T
    lane = lax.broadcasted_iota(jnp.int32, (r_rows, LANE), 1)
    xw = sx * wgt_x
    da_hi, da_lo = _split(da_x)
    ones_n = jnp.ones((r_rows, SSD_N), bf16)
    ys, new_h = [], []
    for grp in range(SSD_GROUPS):
        bm = xbc[:, D + grp * SSD_N:D + (grp + 1) * SSD_N]
        cm = xbc[:, D + (SSD_GROUPS + grp) * SSD_N:D + (SSD_GROUPS + grp + 1) * SSD_N]
        cb = _dot_nt(cm, bm)
        gs = slice(grp * SSD_HP, (grp + 1) * SSD_HP)
        y_inter = _bdot(cm, h_list[grp], nb, c, "nt") * ecum_x[:, gs]
        pairs = []
        for pr in range(SSD_HP // LANE):
            col = grp * SSD_HP + pr * LANE
            xp = sx[:, col:col + LANE]
            acc = None
            for half in range(2):
                hd = (col // SSD_P) + half
                seg = cum[:, hd:hd + 1] - cum_t[hd:hd + 1, :]
                m = cb * jnp.exp(jnp.where(causal, seg, -jnp.inf)) * dt_t[hd:hd + 1, :]
                xh = jnp.where((lane < SSD_P) if half == 0 else (lane >= SSD_P), xp, 0.0)
                part = _dot(m, xh)
                acc = part if acc is None else acc + part
            pairs.append(acc)
        ys.append(jnp.concatenate(pairs, axis=1) + y_inter)
        dec = jnp.exp(_bdot(da_hi[:, gs], ones_n, nb, c, "tn") + _bdot(da_lo[:, gs], ones_n, nb, c, "tn"))
        new_h.append(dec * h_list[grp] + _bdot(xw[:, gs], bm, nb, c, "tn"))
    y = jnp.concatenate(ys, axis=1) + dskip_ref[...] * sx
    y = y * jax.nn.silu(sz)
    halves = []
    for grp in range(SSD_GROUPS):
        yh = y[:, grp * SSD_HP:(grp + 1) * SSD_HP]
        halves.append(yh * lax.rsqrt(jnp.mean(yh * yh, axis=-1, keepdims=True) + EPS))
    return jnp.concatenate(halves, axis=1) * nrm_ref[...], new_h


def _ssd_sample_kernel(u_ref, bp_ref, h0_ref, cw_ref, cb_ref, dtb_ref, alog_ref, dskip_ref, nrm_ref, exp_ref,
                       y_ref, h_ref, *, nb, c):
    sz = u_ref[:, O_SSD:O_SSD + D]
    raw = u_ref[:, O_SSD + D:O_SSD + D + SSD_CH]
    sdt = u_ref[:, O_SSD + D + SSD_CH:O_SSD + W_SSD]
    xbc = jax.nn.silu(_conv_groups(raw, bp_ref[...], cw_ref, cb_ref))
    hpg = SSD_HEADS // SSD_GROUPS
    h_list = [h0_ref[:, grp * hpg:(grp + 1) * hpg].reshape(nb, SSD_HP, SSD_N) for grp in range(SSD_GROUPS)]
    y, new_h = _ssd_core(sz, xbc, sdt, h_list, nb, c, (dtb_ref, alog_ref, dskip_ref, nrm_ref, exp_ref))
    y_ref[...] = y
    for grp in range(SSD_GROUPS):
        h_ref[:, grp * hpg:(grp + 1) * hpg] = new_h[grp].reshape(nb, hpg, SSD_P, SSD_N)


def _ssd_param_specs(layer):
    return [_layer_spec((CONV_W, SSD_CH), layer), _layer_spec((1, SSD_CH), layer), _layer_spec((1, LANE), layer),
            _layer_spec((1, LANE), layer), _layer_spec((1, D), layer), _layer_spec((1, D), layer),
            _const_spec((LANE, D))]


def _ssd_sample(u, bufpad, h0_all, params, layer, nb, c):
    nseq = h0_all.shape[1]
    blk = (nb, SSD_HEADS, SSD_P, SSD_N)
    row = lambda i: (i, 0)
    return pl.pallas_call(
        functools.partial(_ssd_sample_kernel, nb=nb, c=c),
        name="ssd_sample",
        out_shape=(jax.ShapeDtypeStruct((nseq * c, D), f32), jax.ShapeDtypeStruct(h0_all.shape[1:], f32)),
        grid=(nseq // nb,),
        in_specs=[pl.BlockSpec((nb * c, W_MIX), row), pl.BlockSpec((nb * c, SSD_CH), row),
                  pl.BlockSpec((None,) + blk, lambda i: (layer, i, 0, 0, 0))] + _ssd_param_specs(layer),
        out_specs=(pl.BlockSpec((nb * c, D), row), pl.BlockSpec(blk, lambda i: (i, 0, 0, 0))),
        compiler_params=_cp(1),
    )(u, bufpad, h0_all, *params)


def _mix_prompt_kernel(x_ref, g_ref, w_ref, *refs):
    lru_p = refs[0:7]
    gla_p = refs[7:10]
    ssd_cw, ssd_cb = refs[10:12]
    ssd_p = refs[12:17]
    yl_ref, yg_ref, ys_ref, lh_ref, ltail_ref, gs_ref, sh_ref, stail_ref = refs[17:25]
    hc_scr, ltail_scr, s_scr, h_scr, stail_scr = refs[25:30]
    t = pl.program_id(1)

    @pl.when(t == 0)
    def _():
        hc_scr[...] = jnp.zeros_like(hc_scr)
        ltail_scr[...] = jnp.zeros_like(ltail_scr)
        s_scr[...] = jnp.zeros_like(s_scr)
        h_scr[...] = jnp.zeros_like(h_scr)
        stail_scr[...] = jnp.zeros_like(stail_scr)

    hn = _rms(x_ref[...], g_ref[...]).astype(bf16)
    rows = hn.shape[0]

    _lru_prompt_tiles(hn, w_ref, lru_p, yl_ref, lh_ref, ltail_ref, hc_scr, ltail_scr)

    u = jnp.dot(hn, w_ref[:, O_GLA:O_GLA + W_GLA], preferred_element_type=f32)
    s_list = [s_scr[h][None] for h in range(GLA_HEADS)]
    o, s_list = _gla_core(u, s_list, rows // GLA_CHUNK, GLA_CHUNK, gla_p, chain=True)
    yg_ref[...] = o
    for h in range(GLA_HEADS):
        s_scr[h] = s_list[h][0]

    u = jnp.dot(hn, w_ref[:, O_SSD:O_SSD + W_SSD], preferred_element_type=f32)
    sz = u[:, :D]
    raw = u[:, D:D + SSD_CH]
    sdt = u[:, D + SSD_CH:]
    xbc = jax.nn.silu(_conv_carry(raw, stail_scr[...], ssd_cw, ssd_cb))
    h_list = [h_scr[grp][None] for grp in range(SSD_GROUPS)]
    c = SSD_CHUNK
    for j in range(rows // c):
        rs = slice(j * c, (j + 1) * c)
        y, h_list = _ssd_core(sz[rs], xbc[rs], sdt[rs], h_list, 1, c, ssd_p)
        ys_ref[rs, :] = y
    for grp in range(SSD_GROUPS):
        h_scr[grp] = h_list[grp][0]
    stail = raw[rows - SUB:]
    stail_scr[...] = stail
    stail_ref[0] = stail

    @pl.when(t == pl.num_programs(1) - 1)
    def _():
        gs_ref[0] = s_scr[...]
        hpg = SSD_HEADS // SSD_GROUPS
        for grp in range(SSD_GROUPS):
            sh_ref[0, grp * hpg:(grp + 1) * hpg] = h_scr[grp].reshape(hpg, SSD_P, SSD_N)


def _mix_prompt(x, g, w, lru_params, gla_params, ssd_params, layer, nseq, seqlen, tt):
    nt = seqlen // tt
    row = lambda b, t: (b * nt + t, 0)
    seq3 = lambda b, t: (b, 0, 0)
    seq4 = lambda b, t: (b, 0, 0, 0)
    rows = nseq * seqlen
    return pl.pallas_call(
        _mix_prompt_kernel,
        name="mix_prompt",
        out_shape=(jax.ShapeDtypeStruct((rows, D), f32), jax.ShapeDtypeStruct((rows, D), f32),
                   jax.ShapeDtypeStruct((rows, D), f32),
                   jax.ShapeDtypeStruct((nseq, SUB, D), f32), jax.ShapeDtypeStruct((nseq, SUB, D), f32),
                   jax.ShapeDtypeStruct((nseq, GLA_HEADS, GLA_DK, GLA_DV), f32),
                   jax.ShapeDtypeStruct((nseq, SSD_HEADS, SSD_P, SSD_N), f32),
                   jax.ShapeDtypeStruct((nseq, SUB, SSD_CH), f32)),
        grid=(nseq, nt),
        in_specs=[pl.BlockSpec((tt, D), row), _layer_spec((1, D), layer), _layer_spec((D, W_MIX), layer)]
        + _lru_param_specs(layer) + _gla_param_specs(layer) + _ssd_param_specs(layer),
        out_specs=(pl.BlockSpec((tt, D), row), pl.BlockSpec((tt, D), row), pl.BlockSpec((tt, D), row),
                   pl.BlockSpec((1, SUB, D), seq3), pl.BlockSpec((1, SUB, D), seq3),
                   pl.BlockSpec((1, GLA_HEADS, GLA_DK, GLA_DV), seq4),
                   pl.BlockSpec((1, SSD_HEADS, SSD_P, SSD_N), seq4),
                   pl.BlockSpec((1, SUB, SSD_CH), seq3)),
        scratch_shapes=[pltpu.VMEM((SUB, D), f32), pltpu.VMEM((SUB, D), f32),
                        pltpu.VMEM((GLA_HEADS, GLA_DK, GLA_DV), f32),
                        pltpu.VMEM((SSD_GROUPS, SSD_HP, SSD_N), f32), pltpu.VMEM((SUB, SSD_CH), f32)],
        compiler_params=_cp(2),
    )(x, g, w, *lru_params, *gla_params, *ssd_params)


def _merge_kernel(x_ref, g_ref, wg_ref, yl_ref, yg_ref, ys_ref, wl_ref, wgl_ref, ws_ref, wm_ref, o_ref):
    x = x_ref[...]
    gates = jax.nn.sigmoid(_dot(_rms(x, g_ref[...]), wg_ref[...]))
    merged = gates[:, 0:D] * _dot(yl_ref[...], wl_ref[...])
    merged = merged + gates[:, D:2 * D] * _dot(yg_ref[...], wgl_ref[...])
    merged = merged + gates[:, 2 * D:] * _dot(ys_ref[...], ws_ref[...])
    o_ref[...] = x + _dot(merged, wm_ref[...])


def _merge(name, x, g, wgate, yl, yg, ys, wl, wgl, ws, wm, layer, tm):
    m = x.shape[0]
    row = pl.BlockSpec((tm, D), lambda i: (i, 0))
    wsq = _layer_spec((D, D), layer)
    return pl.pallas_call(
        _merge_kernel,
        name=name,
        out_shape=jax.ShapeDtypeStruct((m, D), f32),
        grid=(m // tm,),
        in_specs=[row, _layer_spec((1, D), layer), _layer_spec((D, W_GATE), layer), row, row, row,
                  wsq, wsq, wsq, wsq],
        out_specs=row,
        compiler_params=_cp(1),
    )(x, g, wgate, yl, yg, ys, wl, wgl, ws, wm)


def _attend(q, get_k, get_v, nb, rows):
    outs = []
    for h in range(XA_HEADS):
        hs = slice(h * XA_HD, (h + 1) * XA_HD)
        s = _bdot(q[:, hs], get_k(h), nb, rows, "nt") * XA_HD ** -0.5
        s = s - jnp.max(s, axis=-1, keepdims=True)
        e = jnp.exp(s)
        p = e / jnp.sum(e, axis=-1, keepdims=True)
        outs.append(_bdot(p, get_v(h), nb, rows, "nn"))
    return jnp.concatenate(outs, axis=1)


def _xattn_prompt_kernel(x_ref, g_ref, wq_ref, k_ref, v_ref, wo_ref, o_ref):
    x = x_ref[...]
    q = _dot(_rms(x, g_ref[...]), wq_ref[...])
    o = _attend(q, lambda h: k_ref[:, :, h * XA_HD:(h + 1) * XA_HD],
                lambda h: v_ref[:, :, h * XA_HD:(h + 1) * XA_HD], 1, x.shape[0])
    o_ref[...] = x + _dot(o, wo_ref[...])


def _xattn_sample_kernel(x_ref, g_ref, wq_ref, k_ref, v_ref, wo_ref, o_ref, *, nb, rows):
    x = x_ref[...]
    q = _dot(_rms(x, g_ref[...]), wq_ref[...]).reshape(nb, rows, D)
    qh = jnp.concatenate([q[:, :, h * XA_HD:(h + 1) * XA_HD] for h in range(XA_HEADS)], axis=1)
    k2 = k_ref[...].reshape(nb, MEM * XA_HEADS, XA_HD).astype(bf16)
    v2 = v_ref[...].reshape(nb, MEM * XA_HEADS, XA_HD).astype(bf16)
    s = jnp.einsum("bik,bjk->bij", qh.astype(bf16), k2, preferred_element_type=f32) * XA_HD ** -0.5
    shp = (XA_HEADS * rows, MEM * XA_HEADS)
    q_head = lax.broadcasted_iota(jnp.int32, shp, 0) // rows
    m_head = lax.broadcasted_iota(jnp.int32, shp, 1) & (XA_HEADS - 1)
    s = jnp.where((q_head == m_head)[None], s, -jnp.inf)
    s = s - jnp.max(s, axis=-1, keepdims=True)
    e = jnp.exp(s)
    p = e / jnp.sum(e, axis=-1, keepdims=True)
    o2 = jnp.einsum("bij,bjk->bik", p.astype(bf16), v2, preferred_element_type=f32)
    o = jnp.concatenate([o2[:, h * rows:(h + 1) * rows, :] for h in range(XA_HEADS)], axis=2)
    o_ref[...] = x + _dot(o.reshape(nb * rows, D), wo_ref[...])


def _xattn_prompt(x, g, wq, kv, wo, layer, nseq, seqlen, tq):
    nt = seqlen // tq
    row = pl.BlockSpec((tq, D), lambda b, t: (b * nt + t, 0))
    return pl.pallas_call(
        _xattn_prompt_kernel,
        name="xattn_prompt",
        out_shape=jax.ShapeDtypeStruct(x.shape, f32),
        grid=(nseq, nt),
        in_specs=[row, _layer_spec((1, D), layer), _layer_spec((D, D), layer),
                  pl.BlockSpec((1, MEM, D), lambda b, t: (b, 0, 0)),
                  pl.BlockSpec((1, MEM, D), lambda b, t: (b, 0, 1)), _layer_spec((D, D), layer)],
        out_specs=row,
        compiler_params=_cp(2),
    )(x, g, wq, kv, kv, wo)


def _xattn_sample(x, g, wq, k_all, v_all, wo, layer, nb, rows):
    nseq = k_all.shape[1]
    row = pl.BlockSpec((nb * rows, D), lambda i: (i, 0))
    kvspec = pl.BlockSpec((None, nb, MEM, XA_HEADS, XA_HD), lambda i: (layer, i, 0, 0, 0))
    return pl.pallas_call(
        functools.partial(_xattn_sample_kernel, nb=nb, rows=rows),
        name="xattn_sample",
        out_shape=jax.ShapeDtypeStruct(x.shape, f32),
        grid=(nseq // nb,),
        in_specs=[row, _layer_spec((1, D), layer), _layer_spec((D, D), layer), kvspec, kvspec,
                  _layer_spec((D, D), layer)],
        out_specs=row,
        compiler_params=_cp(1),
    )(x, g, wq, k_all, v_all, wo)


FF_SPLIT = 2


def _ffn_kernel(x_ref, g_ref, wg_ref, wu_ref, wd_ref, o_ref):
    x = x_ref[...]
    h = _rms(x, g_ref[...]).astype(bf16)
    fc = D_FF // FF_SPLIT
    acc = x
    for j in range(FF_SPLIT):
        cs = slice(j * fc, (j + 1) * fc)
        gate = jnp.dot(h, wg_ref[:, cs], preferred_element_type=f32)
        up = jnp.dot(h, wu_ref[:, cs], preferred_element_type=f32)
        acc = acc + _dot(jax.nn.silu(gate) * up, wd_ref[cs, :])
    o_ref[...] = acc


def _ffn(name, x, g, wg, wu, wd, layer, tm):
    m = x.shape[0]
    row = pl.BlockSpec((tm, D), lambda i: (i, 0))
    return pl.pallas_call(
        _ffn_kernel,
        name=name,
        out_shape=jax.ShapeDtypeStruct((m, D), f32),
        grid=(m // tm,),
        in_specs=[row, _layer_spec((1, D), layer), _layer_spec((D, D_FF), layer), _layer_spec((D, D_FF), layer),
                  _layer_spec((D_FF, D), layer)],
        out_specs=row,
        compiler_params=_cp(1),
    )(x, g, wg, wu, wd)


def _block_diag_tiles(w):
    depth, nblk, bs, _ = w.shape
    per = LRU_TILE // bs
    w4 = w.reshape(depth, nblk // per, per, bs, bs)
    eye = jnp.eye(per, dtype=w.dtype)
    t = jnp.einsum("ltpij,pq->ltpiqj", w4, eye)
    return t.reshape(depth, nblk // per, LRU_TILE, LRU_TILE).astype(bf16)


def _rows(a):
    return a[:, None, :]


def _pad_last(a, width):
    return jnp.pad(a, [(0, 0)] * (a.ndim - 1) + [(0, width - a.shape[-1])])


def kernel(x_prompt, x_sample, mem_prompt, state_lru_h, state_lru_conv, state_gla_S, state_ssd_h, state_ssd_conv, cache_mem_k, cache_mem_v, norm_mix, w_in, lru_conv_w, lru_conv_b, lru_wa, lru_ba, lru_wx, lru_bx, lru_lambda, w_lru_out, gla_w_alpha, gla_b_alpha, gla_norm, w_gla_out, ssd_conv_w, ssd_conv_b, ssd_dt_bias, ssd_a_log, ssd_d, ssd_norm, w_ssd_out, w_mix_out, norm_xattn, norm_mem, w_xq, w_xk, w_xv, w_xo, norm_ffn, w_ffn_gate, w_ffn_up, w_ffn_down, norm_final):
    bp, lp, _ = x_prompt.shape
    bs, ls, _ = x_sample.shape
    assert ls == SUB, "sample kernels treat each 8-row group as one sequence"
    hk = GLA_HEADS * GLA_DK
    xp = x_prompt.reshape(bp * lp, D)
    xs = x_sample.reshape(bs * ls, D)
    mem = mem_prompt.reshape(bp * MEM, D)

    o1 = 2 * D + 2 * hk + 2 * D + GLA_RANK
    o2 = o1 + D + SSD_CH + SSD_HEADS
    w_in16 = w_in.astype(bf16)
    zpad = jnp.zeros((DEPTH, D, LANE - GLA_RANK), bf16)
    w_mix_in = jnp.concatenate([w_in16[:, :, :o1], zpad, w_in16[:, :, o1:o2], zpad], axis=2)
    w_gate = w_in16[:, :, o2:]
    g_mix, g_xa, g_mem, g_ffn = _rows(norm_mix), _rows(norm_xattn), _rows(norm_mem), _rows(norm_ffn)
    lru_params = (lru_conv_w, _rows(lru_conv_b), _rows(lru_lambda), _block_diag_tiles(lru_wa), _rows(lru_ba),
                  _block_diag_tiles(lru_wx), _rows(lru_bx))
    gla_params = (jnp.pad(gla_w_alpha, ((0, 0), (0, LANE - GLA_RANK), (0, 0))).astype(bf16),
                  _rows(gla_b_alpha), _rows(gla_norm))
    expand = (jnp.arange(LANE)[:, None] == (jnp.arange(D)[None, :] // SSD_P)).astype(bf16)
    ssd_params = (ssd_conv_w, _rows(ssd_conv_b), _rows(_pad_last(ssd_dt_bias, LANE)),
                  _rows(_pad_last(ssd_a_log, LANE)), _rows(jnp.repeat(ssd_d, SSD_P, axis=1)),
                  _rows(ssd_norm), expand)
    w_lo, w_go, w_so, w_mo = (w.astype(bf16) for w in (w_lru_out, w_gla_out, w_ssd_out, w_mix_out))
    wq, wo = w_xq.astype(bf16), w_xo.astype(bf16)
    wkv = jnp.concatenate([w_xk, w_xv], axis=2).astype(bf16)
    wfg, wfu, wfd = w_ffn_gate.astype(bf16), w_ffn_up.astype(bf16), w_ffn_down.astype(bf16)
    pad5 = ((0, 0), (0, 0), (SUB - (CONV_W - 1), 0), (0, 0))
    bp_lru = jnp.pad(state_lru_conv, pad5).reshape(DEPTH, bs * ls, D)
    bp_ssd = jnp.pad(state_ssd_conv, pad5).reshape(DEPTH, bs * ls, SSD_CH)
    h0_lru = state_lru_h[:, :, None, :]

    p_out = [[] for _ in range(7)]
    s_out = [[] for _ in range(5)]
    for l in range(DEPTH):
        kv = _norm_matmul("mem_kv", mem, g_mem, wkv, l, T_ROW)
        yl, yg, ys, p_h, p_tail, p_s, p_sh, p_stail = _mix_prompt(
            xp, g_mix, w_mix_in, lru_params, gla_params, ssd_params, l, bp, lp, T_MIX)
        xp = _merge("merge_prompt", xp, g_mix, w_gate, yl, yg, ys, w_lo, w_go, w_so, w_mo, l, T_MERGE)
        xp = _xattn_prompt(xp, g_xa, wq, kv.reshape(bp, MEM, 2 * D), wo, l, bp, lp, T_ROW)
        xp = _ffn("ffn_prompt", xp, g_ffn, wfg, wfu, wfd, l, T_FFN)
        p_vals = (p_h[:, 0], p_tail[:, SUB - 3:], p_s, p_sh, p_stail[:, SUB - 3:],
                  kv[:, :D].reshape(bp, MEM, XA_HEADS, XA_HD), kv[:, D:].reshape(bp, MEM, XA_HEADS, XA_HD))
        for lst, val in zip(p_out, p_vals):
            lst.append(val)

        u = _norm_matmul("in_proj_sample", xs, g_mix, w_mix_in, l, T_SAMPLE)
        yl, h_last = _lru_sample(u, h0_lru, bp_lru[l], lru_params, l, T_SAMPLE)
        yg, s_s = _gla_sample(u, state_gla_S, gla_params, l, NB_STATE, ls)
        ys, s_sh = _ssd_sample(u, bp_ssd[l], state_ssd_h, ssd_params, l, NB_STATE, ls)
        xs = _merge("merge_sample", xs, g_mix, w_gate, yl, yg, ys, w_lo, w_go, w_so, w_mo, l, T_SAMPLE)
        xs = _xattn_sample(xs, g_xa, wq, cache_mem_k, cache_mem_v, wo, l, NB_STATE, ls)
        xs = _ffn("ffn_sample", xs, g_ffn, wfg, wfu, wfd, l, T_SAMPLE)
        u3 = u.reshape(bs, ls, W_MIX)
        s_vals = (h_last[:, 0], u3[:, ls - 3:, :D], s_s, s_sh,
                  u3[:, ls - 3:, O_SSD + D:O_SSD + D + SSD_CH])
        for lst, val in zip(s_out, s_vals):
            lst.append(val)

    y_prompt = _final_norm("final_norm_prompt", xp, norm_final.reshape(1, D), T_ROW).reshape(bp, lp, D)
    y_sample = _final_norm("final_norm_sample", xs, norm_final.reshape(1, D), T_SAMPLE).reshape(bs, ls, D)
    p_stack = [jnp.stack(t, axis=0) for t in p_out]
    s_stack = [jnp.stack(t, axis=0) for t in s_out]
    return (y_prompt, y_sample, *p_stack, *s_stack)
```

```python
import functools

import jax
import jax.numpy as jnp
from jax import lax
from jax.experimental import pallas as pl
from jax.experimental.pallas import tpu as pltpu

f32 = jnp.float32
bf16 = jnp.bfloat16

D = 1024
DEPTH = 4
EPS = 1e-6
CONV_W = 4
LRU_C = 8.0
LRU_TILE = 256
GLA_HEADS, GLA_DK, GLA_DV, GLA_RANK, GLA_TAU, GLA_CHUNK = 4, 128, 256, 16, 16.0, 64
SSD_HEADS, SSD_P, SSD_GROUPS, SSD_N, SSD_CHUNK = 16, 64, 2, 64, 128
SSD_HP = SSD_HEADS // SSD_GROUPS * SSD_P
SSD_CH = D + 2 * SSD_GROUPS * SSD_N
XA_HEADS, XA_HD, MEM = 4, 256, 256
D_FF = 2816
LANE = 128
SUB = 8
W_LRU, W_GLA, W_SSD, W_GATE = 2 * D, 3 * D + LANE, D + SSD_CH + LANE, 3 * D
W_MIX = W_LRU + W_GLA + W_SSD
O_GLA, O_SSD = W_LRU, W_LRU + W_GLA
VMEM_LIMIT = 56 * 1024 * 1024

T_MIX = 256
T_ROW = 512
T_XATTN = 1024
T_FFN = 1024
T_MERGE = 512
T_SAMPLE = 256
NB_STATE = 8


def _cp(n_axes):
    return pltpu.CompilerParams(dimension_semantics=("arbitrary",) * n_axes,
                                vmem_limit_bytes=VMEM_LIMIT)


def _const_spec(shape):
    nd = len(shape)
    return pl.BlockSpec(shape, lambda *_: (0,) * nd, pipeline_mode=pl.Buffered(1))


def _layer_spec(shape, layer):
    nd = len(shape)
    return pl.BlockSpec((None,) + tuple(shape), lambda *_: (layer,) + (0,) * nd,
                        pipeline_mode=pl.Buffered(1))


def _rms(x, g):
    return x * lax.rsqrt(jnp.mean(x * x, axis=-1, keepdims=True) + EPS) * g


def _dot(a, b):
    return jnp.dot(a.astype(bf16), b.astype(bf16), preferred_element_type=f32)


def _dot_nt(a, b):
    return lax.dot_general(a.astype(bf16), b.astype(bf16), (((1,), (1,)), ((), ())),
                           preferred_element_type=f32)


def _split(x):
    hi = x.astype(bf16)
    return hi, (x - hi.astype(f32)).astype(bf16)


def _dot_split(m, x):
    hi, lo = _split(x)
    return (jnp.dot(m, hi, preferred_element_type=f32) + jnp.dot(m, lo, preferred_element_type=f32))


def _dot_split_rows(x, m):
    hi, lo = _split(x)
    return jnp.dot(hi, m, preferred_element_type=f32) + jnp.dot(lo, m, preferred_element_type=f32)


def _bdot(a, b, nb, rows, mode):
    a = a.astype(bf16)
    b = b.astype(bf16)
    if mode == "tn":
        if nb == 1:
            return lax.dot_general(a, b, (((0,), (0,)), ((), ())), preferred_element_type=f32)[None]
        a3 = a.reshape(nb, rows, a.shape[-1])
        b3 = b.reshape(nb, rows, b.shape[-1])
        return jnp.einsum("bki,bkj->bij", a3, b3, preferred_element_type=f32)
    if nb == 1:
        if mode == "nn":
            return jnp.dot(a, b[0], preferred_element_type=f32)
        return lax.dot_general(a, b[0], (((1,), (1,)), ((), ())), preferred_element_type=f32)
    a3 = a.reshape(nb, rows, a.shape[-1])
    eq = "bik,bkj->bij" if mode == "nn" else "bik,bjk->bij"
    out = jnp.einsum(eq, a3, b, preferred_element_type=f32)
    return out.reshape(nb * rows, out.shape[-1])


def _seq_masks(r, c):
    ri = lax.broadcasted_iota(jnp.int32, (r, r), 0)
    ci = lax.broadcasted_iota(jnp.int32, (r, r), 1)
    if r == c:
        same = jnp.full((r, r), True)
    else:
        shift = c.bit_length() - 1
        same = (ri >> shift) == (ci >> shift)
    causal = same & (ri >= ci)
    return causal, same


def _conv_carry(x, tail, w_ref, b_ref):
    xe = jnp.concatenate([tail, x], axis=0)
    y = b_ref[...] + w_ref[0:1, :] * pltpu.roll(xe, 3, 0)[SUB:]
    y = y + w_ref[1:2, :] * pltpu.roll(xe, 2, 0)[SUB:]
    y = y + w_ref[2:3, :] * pltpu.roll(xe, 1, 0)[SUB:]
    return y + w_ref[3:4, :] * x


def _conv_groups(x, bufpad, w_ref, b_ref):
    rows = x.shape[0]
    l = lax.broadcasted_iota(jnp.int32, (rows, 1), 0) & (SUB - 1)
    y = b_ref[...]
    for k in range(CONV_W - 1):
        s = CONV_W - 1 - k
        sh = jnp.where(l >= s, pltpu.roll(x, s, 0), pltpu.roll(bufpad, rows - SUB + s, 0))
        y = y + w_ref[k:k + 1, :] * sh
    return y + w_ref[3:4, :] * x


def _norm_matmul_kernel(x_ref, g_ref, w_ref, o_ref):
    o_ref[...] = _dot(_rms(x_ref[...], g_ref[...]), w_ref[...])


def _norm_matmul(name, x, g, w, layer, tm):
    m, k = x.shape
    n = w.shape[-1]
    return pl.pallas_call(
        _norm_matmul_kernel,
        name=name,
        out_shape=jax.ShapeDtypeStruct((m, n), f32),
        grid=(m // tm,),
        in_specs=[pl.BlockSpec((tm, k), lambda i: (i, 0)), _layer_spec((1, k), layer), _layer_spec((k, n), layer)],
        out_specs=pl.BlockSpec((tm, n), lambda i: (i, 0)),
        compiler_params=_cp(1),
    )(x, g, w)


def _final_norm_kernel(x_ref, g_ref, o_ref):
    o_ref[...] = _rms(x_ref[...], g_ref[...])


def _final_norm(name, x, g, tm):
    m, k = x.shape
    return pl.pallas_call(
        _final_norm_kernel,
        name=name,
        out_shape=jax.ShapeDtypeStruct((m, k), f32),
        grid=(m // tm,),
        in_specs=[pl.BlockSpec((tm, k), lambda i: (i, 0)), _const_spec((1, k))],
        out_specs=pl.BlockSpec((tm, k), lambda i: (i, 0)),
        compiler_params=_cp(1),
    )(x, g)


def _lru_decay_and_scale(log_a):
    a = jnp.exp(log_a)
    m = 1.0 - a * a
    return a, jnp.where(m > 0.0, m * lax.rsqrt(m), 0.0)


def _lru_math(xc, lam_ref, wa_ref, ba_ref, wx_ref, bx_ref):
    xcb = xc.astype(bf16)
    nt = D // LRU_TILE
    r_pre = jnp.concatenate([jnp.dot(xcb[:, i * LRU_TILE:(i + 1) * LRU_TILE], wa_ref[i],
                                     preferred_element_type=f32) for i in range(nt)], axis=1)
    i_pre = jnp.concatenate([jnp.dot(xcb[:, i * LRU_TILE:(i + 1) * LRU_TILE], wx_ref[i],
                                     preferred_element_type=f32) for i in range(nt)], axis=1)
    r = jax.nn.sigmoid(r_pre + ba_ref[...])
    ig = jax.nn.sigmoid(i_pre + bx_ref[...])
    a, mult = _lru_decay_and_scale(-LRU_C * r * jax.nn.softplus(-lam_ref[...]))
    return a, mult * (ig * xc)


def _group_scan(a, b):
    l = lax.broadcasted_iota(jnp.int32, (a.shape[0], 1), 0) & (SUB - 1)
    for s in (1, 2, 4):
        m = l >= s
        b = jnp.where(m, a * pltpu.roll(b, s, 0) + b, b)
        a = jnp.where(m, a * pltpu.roll(a, s, 0), a)
    return a, b


def _lru_prompt_tiles(hn, w_ref, p, yl_ref, lh_ref, ltail_ref, hc_scr, ltail_scr):
    cw_ref, cb_ref, lam_ref, wa_ref, ba_ref, wx_ref, bx_ref = p
    rows = hn.shape[0]
    for ci in range(D // LRU_TILE):
        cs = slice(ci * LRU_TILE, (ci + 1) * LRU_TILE)
        x = jnp.dot(hn, w_ref[:, cs], preferred_element_type=f32)
        gate = jnp.dot(hn, w_ref[:, D + ci * LRU_TILE:D + (ci + 1) * LRU_TILE], preferred_element_type=f32)
        xe = jnp.concatenate([ltail_scr[:, cs], x], axis=0)
        xc = cb_ref[:, cs] + cw_ref[0:1, cs] * pltpu.roll(xe, 3, 0)[SUB:]
        xc = xc + cw_ref[1:2, cs] * pltpu.roll(xe, 2, 0)[SUB:]
        xc = xc + cw_ref[2:3, cs] * pltpu.roll(xe, 1, 0)[SUB:]
        xc = xc + cw_ref[3:4, cs] * x
        xcb = xc.astype(bf16)
        r = jax.nn.sigmoid(jnp.dot(xcb, wa_ref[ci], preferred_element_type=f32) + ba_ref[:, cs])
        ig = jax.nn.sigmoid(jnp.dot(xcb, wx_ref[ci], preferred_element_type=f32) + bx_ref[:, cs])
        a, mult = _lru_decay_and_scale(-LRU_C * r * jax.nn.softplus(-lam_ref[:, cs]))
        b = mult * (ig * xc)
        a, b = _group_scan(a, b)
        carry = hc_scr[0:1, cs]
        hs = []
        for j in range(rows // SUB):
            hj = a[j * SUB:(j + 1) * SUB] * carry + b[j * SUB:(j + 1) * SUB]
            carry = hj[SUB - 1:SUB]
            hs.append(hj)
        yl_ref[:, cs] = jnp.concatenate(hs, axis=0) * jax.nn.gelu(gate)
        hc = jnp.broadcast_to(carry, (SUB, LRU_TILE))
        hc_scr[:, cs] = hc
        lh_ref[0, :, cs] = hc
        ltail = x[rows - SUB:]
        ltail_scr[:, cs] = ltail
        ltail_ref[0, :, cs] = ltail


def _lru_sample_kernel(u_ref, h0_ref, bp_ref, cw_ref, cb_ref, lam_ref, wa_ref, ba_ref, wx_ref, bx_ref,
                       y_ref, h_ref):
    x = u_ref[:, 0:D]
    gate = u_ref[:, D:W_LRU]
    xc = _conv_groups(x, bp_ref[...], cw_ref, cb_ref)
    a, b = _lru_math(xc, lam_ref, wa_ref, ba_ref, wx_ref, bx_ref)
    a, b = _group_scan(a, b)
    nseq = x.shape[0] // SUB
    h3 = a.reshape(nseq, SUB, D) * h0_ref[...] + b.reshape(nseq, SUB, D)
    h_ref[...] = h3[:, SUB - 1:SUB, :]
    y_ref[...] = h3.reshape(nseq * SUB, D) * jax.nn.gelu(gate)


def _lru_param_specs(layer):
    nt = D // LRU_TILE
    return [_layer_spec((CONV_W, D), layer), _layer_spec((1, D), layer), _layer_spec((1, D), layer),
            _layer_spec((nt, LRU_TILE, LRU_TILE), layer), _layer_spec((1, D), layer),
            _layer_spec((nt, LRU_TILE, LRU_TILE), layer), _layer_spec((1, D), layer)]


def _lru_sample(u, h0, bufpad, params, layer, tm):
    m = u.shape[0]
    row = lambda i: (i, 0)
    seq3 = lambda i: (i, 0, 0)
    return pl.pallas_call(
        _lru_sample_kernel,
        name="lru_sample",
        out_shape=(jax.ShapeDtypeStruct((m, D), f32), jax.ShapeDtypeStruct((m // SUB, 1, D), f32)),
        grid=(m // tm,),
        in_specs=[pl.BlockSpec((tm, W_MIX), row),
                  pl.BlockSpec((None, tm // SUB, 1, D), lambda i: (layer, i, 0, 0)),
                  pl.BlockSpec((tm, D), row)] + _lru_param_specs(layer),
        out_specs=(pl.BlockSpec((tm, D), row), pl.BlockSpec((tm // SUB, 1, D), seq3)),
        compiler_params=_cp(1),
    )(u, h0, bufpad, *params)


def _gla_core(u, s_list, nb, c, p, chain=False):
    wal_ref, bal_ref, gn_ref = p
    r_rows = nb * c
    hk = GLA_HEADS * GLA_DK
    q = u[:, 0:hk]
    k = u[:, hk:2 * hk]
    v = u[:, 2 * hk:2 * hk + D]
    rg = u[:, 2 * hk + D:2 * hk + 2 * D]
    glow = u[:, 2 * hk + 2 * D:]
    z = _dot(glow, wal_ref[...]) + bal_ref[...]
    g = jax.nn.log_sigmoid(z) / GLA_TAU
    causal, same = _seq_masks(r_rows, c)
    g_hi, g_lo = _split(g)
    tril = causal.astype(bf16)
    ones_sq = same.astype(bf16)
    bcum = jnp.dot(tril, g_hi, preferred_element_type=f32) + jnp.dot(tril, g_lo, preferred_element_type=f32)
    btot = jnp.dot(ones_sq, g_hi, preferred_element_type=f32) + jnp.dot(ones_sq, g_lo, preferred_element_type=f32)
    qt = (q * GLA_DK ** -0.5) * jnp.exp(bcum)
    kt = k * jnp.exp(-bcum)
    ke = k * jnp.exp(btot - bcum)
    ones_v = jnp.ones((r_rows, GLA_DV), bf16)
    if chain:
        assert nb <= SUB
        tot8 = jnp.concatenate([btot[j * c:j * c + 1, :] for j in range(nb)] + [btot[0:SUB - nb, :]], axis=0)
        dec_t = jnp.exp(tot8.T)
    outs, new_s = [], []
    for h in range(GLA_HEADS):
        ks = slice(h * GLA_DK, (h + 1) * GLA_DK)
        vs = slice(h * GLA_DV, (h + 1) * GLA_DV)
        s_h = s_list[h]
        att = jnp.where(causal, _dot_nt(qt[:, ks], kt[:, ks]), 0.0)
        upd = _bdot(ke[:, ks], v[:, vs], nb, c, "tn")
        if chain:
            dec = [dec_t[h * GLA_DK:(h + 1) * GLA_DK, j:j + 1] for j in range(nb)]
        else:
            blast = _bdot(g_hi[:, ks], ones_v, nb, c, "tn") + _bdot(g_lo[:, ks], ones_v, nb, c, "tn")
            dec = jnp.exp(blast)
        if chain:
            cur = s_h[0]
            starts = []
            for j in range(nb):
                starts.append(cur)
                cur = dec[j] * cur + upd[j]
            s_in = jnp.stack(starts, axis=0)
            new_s.append(cur[None])
        else:
            s_in = s_h
            new_s.append(dec * s_h + upd)
        o_h = _dot(att, v[:, vs]) + _bdot(qt[:, ks], s_in, nb, c, "nn")
        o_h = o_h * lax.rsqrt(jnp.mean(o_h * o_h, axis=-1, keepdims=True) + EPS) * gn_ref[...]
        outs.append(o_h)
    o = jnp.concatenate(outs, axis=1)
    return o * jax.nn.silu(rg), new_s


def _gla_sample_kernel(u_ref, s0_ref, wal_ref, bal_ref, gn_ref, o_ref, s_ref, *, nb, c):
    s_list = [s0_ref[:, h] for h in range(GLA_HEADS)]
    o, new_s = _gla_core(u_ref[:, O_GLA:O_GLA + W_GLA], s_list, nb, c, (wal_ref, bal_ref, gn_ref))
    o_ref[...] = o
    for h in range(GLA_HEADS):
        s_ref[:, h] = new_s[h]


def _gla_param_specs(layer):
    hk = GLA_HEADS * GLA_DK
    return [_layer_spec((LANE, hk), layer), _layer_spec((1, hk), layer), _layer_spec((1, GLA_DV), layer)]


def _gla_sample(u, s0_all, params, layer, nb, c):
    nseq = s0_all.shape[1]
    blk = (nb, GLA_HEADS, GLA_DK, GLA_DV)
    return pl.pallas_call(
        functools.partial(_gla_sample_kernel, nb=nb, c=c),
        name="gla_sample",
        out_shape=(jax.ShapeDtypeStruct((nseq * c, D), f32), jax.ShapeDtypeStruct(s0_all.shape[1:], f32)),
        grid=(nseq // nb,),
        in_specs=[pl.BlockSpec((nb * c, W_MIX), lambda i: (i, 0)),
                  pl.BlockSpec((None,) + blk, lambda i: (layer, i, 0, 0, 0))] + _gla_param_specs(layer),
        out_specs=(pl.BlockSpec((nb * c, D), lambda i: (i, 0)), pl.BlockSpec(blk, lambda i: (i, 0, 0, 0))),
        compiler_params=_cp(1),
    )(u, s0_all, *params)


def _ssd_core(sz, xbc, sdt, h_list, nb, c, p):
    dtb_ref, alog_ref, dskip_ref, nrm_ref, exp_ref = p
    r_rows = nb * c
    sx = xbc[:, :D]
    dt = jax.nn.softplus(sdt + dtb_ref[...])
    da = dt * (-jnp.exp(alog_ref[...]))
    causal, same = _seq_masks(r_rows, c)
    cum = _dot_split(causal.astype(bf16), da)
    tot = _dot_split(same.astype(bf16), da)
    ecum = jnp.exp(cum)
    wgt = jnp.exp(tot - cum) * dt
    expd = _dot_split_rows(jnp.concatenate([ecum, wgt, da], axis=0), exp_ref[...])
    ecum_x = expd[0:r_rows]
    wgt_x = expd[r_rows:2 * r_rows]
    da_x = expd[2 * r_rows:]
    cum_t = cum.T
    dt_t = dt.T
    lane = lax.broadcasted_iota(jnp.int32, (r_rows, LANE), 1)
    xw = sx * wgt_x
    da_hi, da_lo = _split(da_x)
    ones_n = jnp.ones((r_rows, SSD_N), bf16)
    ys, new_h = [], []
    for grp in range(SSD_GROUPS):
        bm = xbc[:, D + grp * SSD_N:D + (grp + 1) * SSD_N]
        cm = xbc[:, D + (SSD_GROUPS + grp) * SSD_N:D + (SSD_GROUPS + grp + 1) * SSD_N]
        cb = _dot_nt(cm, bm)
        gs = slice(grp * SSD_HP, (grp + 1) * SSD_HP)
        y_inter = _bdot(cm, h_list[grp], nb, c, "nt") * ecum_x[:, gs]
        pairs = []
        for pr in range(SSD_HP // LANE):
            col = grp * SSD_HP + pr * LANE
            xp = sx[:, col:col + LANE]
            acc = None
            for half in range(2):
                hd = (col // SSD_P) + half
                seg = cum[:, hd:hd + 1] - cum_t[hd:hd + 1, :]
                m = cb * jnp.exp(jnp.where(causal, seg, -jnp.inf)) * dt_t[hd:hd + 1, :]
                xh = jnp.where((lane < SSD_P) if half == 0 else (lane >= SSD_P), xp, 0.0)
                part = _dot(m, xh)
                acc = part if acc is None else acc + part
            pairs.append(acc)
        ys.append(jnp.concatenate(pairs, axis=1) + y_inter)
        dec = jnp.exp(_bdot(da_hi[:, gs], ones_n, nb, c, "tn") + _bdot(da_lo[:, gs], ones_n, nb, c, "tn"))
        new_h.append(dec * h_list[grp] + _bdot(xw[:, gs], bm, nb, c, "tn"))
    y = jnp.concatenate(ys, axis=1) + dskip_ref[...] * sx
    y = y * jax.nn.silu(sz)
    halves = []
    for grp in range(SSD_GROUPS):
        yh = y[:, grp * SSD_HP:(grp + 1) * SSD_HP]
        halves.append(yh * lax.rsqrt(jnp.mean(yh * yh, axis=-1, keepdims=True) + EPS))
    return jnp.concatenate(halves, axis=1) * nrm_ref[...], new_h


def _ssd_sample_kernel(u_ref, bp_ref, h0_ref, cw_ref, cb_ref, dtb_ref, alog_ref, dskip_ref, nrm_ref, exp_ref,
                       y_ref, h_ref, *, nb, c):
    sz = u_ref[:, O_SSD:O_SSD + D]
    raw = u_ref[:, O_SSD + D:O_SSD + D + SSD_CH]
    sdt = u_ref[:, O_SSD + D + SSD_CH:O_SSD + W_SSD]
    xbc = jax.nn.silu(_conv_groups(raw, bp_ref[...], cw_ref, cb_ref))
    hpg = SSD_HEADS // SSD_GROUPS
    h_list = [h0_ref[:, grp * hpg:(grp + 1) * hpg].reshape(nb, SSD_HP, SSD_N) for grp in range(SSD_GROUPS)]
    y, new_h = _ssd_core(sz, xbc, sdt, h_list, nb, c, (dtb_ref, alog_ref, dskip_ref, nrm_ref, exp_ref))
    y_ref[...] = y
    for grp in range(SSD_GROUPS):
        h_ref[:, grp * hpg:(grp + 1) * hpg] = new_h[grp].reshape(nb, hpg, SSD_P, SSD_N)


def _ssd_param_specs(layer):
    return [_layer_spec((CONV_W, SSD_CH), layer), _layer_spec((1, SSD_CH), layer), _layer_spec((1, LANE), layer),
            _layer_spec((1, LANE), layer), _layer_spec((1, D), layer), _layer_spec((1, D), layer),
            _const_spec((LANE, D))]


def _ssd_sample(u, bufpad, h0_all, params, layer, nb, c):
    nseq = h0_all.shape[1]
    blk = (nb, SSD_HEADS, SSD_P, SSD_N)
    row = lambda i: (i, 0)
    return pl.pallas_call(
        functools.partial(_ssd_sample_kernel, nb=nb, c=c),
        name="ssd_sample",
        out_shape=(jax.ShapeDtypeStruct((nseq * c, D), f32), jax.ShapeDtypeStruct(h0_all.shape[1:], f32)),
        grid=(nseq // nb,),
        in_specs=[pl.BlockSpec((nb * c, W_MIX), row), pl.BlockSpec((nb * c, SSD_CH), row),
                  pl.BlockSpec((None,) + blk, lambda i: (layer, i, 0, 0, 0))] + _ssd_param_specs(layer),
        out_specs=(pl.BlockSpec((nb * c, D), row), pl.BlockSpec(blk, lambda i: (i, 0, 0, 0))),
        compiler_params=_cp(1),
    )(u, bufpad, h0_all, *params)


def _mix_prompt_kernel(x_ref, g_ref, w_ref, *refs):
    lru_p = refs[0:7]
    gla_p = refs[7:10]
    ssd_cw, ssd_cb = refs[10:12]
    ssd_p = refs[12:17]
    yl_ref, yg_ref, ys_ref, lh_ref, ltail_ref, gs_ref, sh_ref, stail_ref = refs[17:25]
    hc_scr, ltail_scr, s_scr, h_scr, stail_scr = refs[25:30]
    t = pl.program_id(1)

    @pl.when(t == 0)
    def _():
        hc_scr[...] = jnp.zeros_like(hc_scr)
        ltail_scr[...] = jnp.zeros_like(ltail_scr)
        s_scr[...] = jnp.zeros_like(s_scr)
        h_scr[...] = jnp.zeros_like(h_scr)
        stail_scr[...] = jnp.zeros_like(stail_scr)

    hn = _rms(x_ref[...], g_ref[...]).astype(bf16)
    rows = hn.shape[0]

    _lru_prompt_tiles(hn, w_ref, lru_p, yl_ref, lh_ref, ltail_ref, hc_scr, ltail_scr)

    u = jnp.dot(hn, w_ref[:, O_GLA:O_GLA + W_GLA], preferred_element_type=f32)
    s_list = [s_scr[h][None] for h in range(GLA_HEADS)]
    o, s_list = _gla_core(u, s_list, rows // GLA_CHUNK, GLA_CHUNK, gla_p, chain=True)
    yg_ref[...] = o
    for h in range(GLA_HEADS):
        s_scr[h] = s_list[h][0]

    u = jnp.dot(hn, w_ref[:, O_SSD:O_SSD + W_SSD], preferred_element_type=f32)
    sz = u[:, :D]
    raw = u[:, D:D + SSD_CH]
    sdt = u[:, D + SSD_CH:]
    xbc = jax.nn.silu(_conv_carry(raw, stail_scr[...], ssd_cw, ssd_cb))
    h_list = [h_scr[grp][None] for grp in range(SSD_GROUPS)]
    c = SSD_CHUNK
    for j in range(rows // c):
        rs = slice(j * c, (j + 1) * c)
        y, h_list = _ssd_core(sz[rs], xbc[rs], sdt[rs], h_list, 1, c, ssd_p)
        ys_ref[rs, :] = y
    for grp in range(SSD_GROUPS):
        h_scr[grp] = h_list[grp][0]
    stail = raw[rows - SUB:]
    stail_scr[...] = stail
    stail_ref[0] = stail

    @pl.when(t == pl.num_programs(1) - 1)
    def _():
        gs_ref[0] = s_scr[...]
        hpg = SSD_HEADS // SSD_GROUPS
        for grp in range(SSD_GROUPS):
            sh_ref[0, grp * hpg:(grp + 1) * hpg] = h_scr[grp].reshape(hpg, SSD_P, SSD_N)


def _mix_prompt(x, g, w, lru_params, gla_params, ssd_params, layer, nseq, seqlen, tt):
    nt = seqlen // tt
    row = lambda b, t: (b * nt + t, 0)
    seq3 = lambda b, t: (b, 0, 0)
    seq4 = lambda b, t: (b, 0, 0, 0)
    rows = nseq * seqlen
    return pl.pallas_call(
        _mix_prompt_kernel,
        name="mix_prompt",
        out_shape=(jax.ShapeDtypeStruct((rows, D), f32), jax.ShapeDtypeStruct((rows, D), f32),
                   jax.ShapeDtypeStruct((rows, D), f32),
                   jax.ShapeDtypeStruct((nseq, SUB, D), f32), jax.ShapeDtypeStruct((nseq, SUB, D), f32),
                   jax.ShapeDtypeStruct((nseq, GLA_HEADS, GLA_DK, GLA_DV), f32),
                   jax.ShapeDtypeStruct((nseq, SSD_HEADS, SSD_P, SSD_N), f32),
                   jax.ShapeDtypeStruct((nseq, SUB, SSD_CH), f32)),
        grid=(nseq, nt),
        in_specs=[pl.BlockSpec((tt, D), row), _layer_spec((1, D), layer), _layer_spec((D, W_MIX), layer)]
        + _lru_param_specs(layer) + _gla_param_specs(layer) + _ssd_param_specs(layer),
        out_specs=(pl.BlockSpec((tt, D), row), pl.BlockSpec((tt, D), row), pl.BlockSpec((tt, D), row),
                   pl.BlockSpec((1, SUB, D), seq3), pl.BlockSpec((1, SUB, D), seq3),
                   pl.BlockSpec((1, GLA_HEADS, GLA_DK, GLA_DV), seq4),
                   pl.BlockSpec((1, SSD_HEADS, SSD_P, SSD_N), seq4),
                   pl.BlockSpec((1, SUB, SSD_CH), seq3)),
        scratch_shapes=[pltpu.VMEM((SUB, D), f32), pltpu.VMEM((SUB, D), f32),
                        pltpu.VMEM((GLA_HEADS, GLA_DK, GLA_DV), f32),
                        pltpu.VMEM((SSD_GROUPS, SSD_HP, SSD_N), f32), pltpu.VMEM((SUB, SSD_CH), f32)],
        compiler_params=_cp(2),
    )(x, g, w, *lru_params, *gla_params, *ssd_params)


def _merge_kernel(x_ref, g_ref, wg_ref, yl_ref, yg_ref, ys_ref, wl_ref, wgl_ref, ws_ref, wm_ref, o_ref):
    x = x_ref[...]
    gates = jax.nn.sigmoid(_dot(_rms(x, g_ref[...]), wg_ref[...]))
    merged = gates[:, 0:D] * _dot(yl_ref[...], wl_ref[...])
    merged = merged + gates[:, D:2 * D] * _dot(yg_ref[...], wgl_ref[...])
    merged = merged + gates[:, 2 * D:] * _dot(ys_ref[...], ws_ref[...])
    o_ref[...] = x + _dot(merged, wm_ref[...])


def _merge(name, x, g, wgate, yl, yg, ys, wl, wgl, ws, wm, layer, tm):
    m = x.shape[0]
    row = pl.BlockSpec((tm, D), lambda i: (i, 0))
    wsq = _layer_spec((D, D), layer)
    return pl.pallas_call(
        _merge_kernel,
        name=name,
        out_shape=jax.ShapeDtypeStruct((m, D), f32),
        grid=(m // tm,),
        in_specs=[row, _layer_spec((1, D), layer), _layer_spec((D, W_GATE), layer), row, row, row,
                  wsq, wsq, wsq, wsq],
        out_specs=row,
        compiler_params=_cp(1),
    )(x, g, wgate, yl, yg, ys, wl, wgl, ws, wm)


def _attend(q, get_k, get_v, nb, rows):
    outs = []
    for h in range(XA_HEADS):
        hs = slice(h * XA_HD, (h + 1) * XA_HD)
        s = _bdot(q[:, hs], get_k(h), nb, rows, "nt") * XA_HD ** -0.5
        s = s - jnp.max(s, axis=-1, keepdims=True)
        e = jnp.exp(s)
        p = e / jnp.sum(e, axis=-1, keepdims=True)
        outs.append(_bdot(p, get_v(h), nb, rows, "nn"))
    return jnp.concatenate(outs, axis=1)


def _xattn_prompt_kernel(x_ref, g_ref, wq_ref, k_ref, v_ref, wo_ref, o_ref):
    x = x_ref[...]
    q = _dot(_rms(x, g_ref[...]), wq_ref[...])
    o = _attend(q, lambda h: k_ref[:, :, h * XA_HD:(h + 1) * XA_HD],
                lambda h: v_ref[:, :, h * XA_HD:(h + 1) * XA_HD], 1, x.shape[0])
    o_ref[...] = x + _dot(o, wo_ref[...])


def _xattn_sample_kernel(x_ref, g_ref, wq_ref, k_ref, v_ref, wo_ref, o_ref, *, nb, rows):
    x = x_ref[...]
    q = _dot(_rms(x, g_ref[...]), wq_ref[...]).reshape(nb, rows, D)
    qh = jnp.concatenate([q[:, :, h * XA_HD:(h + 1) * XA_HD] for h in range(XA_HEADS)], axis=1)
    k2 = k_ref[...].reshape(nb, MEM * XA_HEADS, XA_HD).astype(bf16)
    v2 = v_ref[...].reshape(nb, MEM * XA_HEADS, XA_HD).astype(bf16)
    s = jnp.einsum("bik,bjk->bij", qh.astype(bf16), k2, preferred_element_type=f32) * XA_HD ** -0.5
    shp = (XA_HEADS * rows, MEM * XA_HEADS)
    q_head = lax.broadcasted_iota(jnp.int32, shp, 0) // rows
    m_head = lax.broadcasted_iota(jnp.int32, shp, 1) & (XA_HEADS - 1)
    s = jnp.where((q_head == m_head)[None], s, -jnp.inf)
    s = s - jnp.max(s, axis=-1, keepdims=True)
    e = jnp.exp(s)
    p = e / jnp.sum(e, axis=-1, keepdims=True)
    o2 = jnp.einsum("bij,bjk->bik", p.astype(bf16), v2, preferred_element_type=f32)
    o = jnp.concatenate([o2[:, h * rows:(h + 1) * rows, :] for h in range(XA_HEADS)], axis=2)
    o_ref[...] = x + _dot(o.reshape(nb * rows, D), wo_ref[...])


def _xattn_prompt(x, g, wq, kv, wo, layer, nseq, seqlen, tq):
    nt = seqlen // tq
    row = pl.BlockSpec((tq, D), lambda b, t: (b * nt + t, 0))
    return pl.pallas_call(
        _xattn_prompt_kernel,
        name="xattn_prompt",
        out_shape=jax.ShapeDtypeStruct(x.shape, f32),
        grid=(nseq, nt),
        in_specs=[row, _layer_spec((1, D), layer), _layer_spec((D, D), layer),
                  pl.BlockSpec((1, MEM, D), lambda b, t: (b, 0, 0)),
                  pl.BlockSpec((1, MEM, D), lambda b, t: (b, 0, 1)), _layer_spec((D, D), layer)],
        out_specs=row,
        compiler_params=_cp(2),
    )(x, g, wq, kv, kv, wo)


def _xattn_sample(x, g, wq, k_all, v_all, wo, layer, nb, rows):
    nseq = k_all.shape[1]
    row = pl.BlockSpec((nb * rows, D), lambda i: (i, 0))
    kvspec = pl.BlockSpec((None, nb, MEM, XA_HEADS, XA_HD), lambda i: (layer, i, 0, 0, 0))
    return pl.pallas_call(
        functools.partial(_xattn_sample_kernel, nb=nb, rows=rows),
        name="xattn_sample",
        out_shape=jax.ShapeDtypeStruct(x.shape, f32),
        grid=(nseq // nb,),
        in_specs=[row, _layer_spec((1, D), layer), _layer_spec((D, D), layer), kvspec, kvspec,
                  _layer_spec((D, D), layer)],
        out_specs=row,
        compiler_params=_cp(1),
    )(x, g, wq, k_all, v_all, wo)


FF_SPLIT = 2


def _ffn_kernel(x_ref, g_ref, wg_ref, wu_ref, wd_ref, o_ref):
    x = x_ref[...]
    h = _rms(x, g_ref[...]).astype(bf16)
    fc = D_FF // FF_SPLIT
    acc = x
    for j in range(FF_SPLIT):
        cs = slice(j * fc, (j + 1) * fc)
        gate = jnp.dot(h, wg_ref[:, cs], preferred_element_type=f32)
        up = jnp.dot(h, wu_ref[:, cs], preferred_element_type=f32)
        acc = acc + _dot(jax.nn.silu(gate) * up, wd_ref[cs, :])
    o_ref[...] = acc


def _ffn(name, x, g, wg, wu, wd, layer, tm):
    m = x.shape[0]
    row = pl.BlockSpec((tm, D), lambda i: (i, 0))
    return pl.pallas_call(
        _ffn_kernel,
        name=name,
        out_shape=jax.ShapeDtypeStruct((m, D), f32),
        grid=(m // tm,),
        in_specs=[row, _layer_spec((1, D), layer), _layer_spec((D, D_FF), layer), _layer_spec((D, D_FF), layer),
                  _layer_spec((D_FF, D), layer)],
        out_specs=row,
        compiler_params=_cp(1),
    )(x, g, wg, wu, wd)


def _prep_w_in_kernel(w_ref, mix_ref, gate_ref, *, o1, o2):
    zpad = jnp.zeros((w_ref.shape[0], LANE - GLA_RANK), bf16)
    lo = O_GLA + W_GLA - LANE
    mix_ref[:, 0:lo] = w_ref[:, 0:lo].astype(bf16)
    mix_ref[:, lo:O_SSD] = jnp.concatenate([w_ref[:, lo:o1].astype(bf16), zpad], axis=1)
    mix_ref[:, O_SSD:W_MIX] = jnp.concatenate([w_ref[:, o1:o2].astype(bf16), zpad], axis=1)
    gate_ref[...] = w_ref[:, o2:].astype(bf16)


def _prep_w_in(w_in, o1, o2):
    depth, k, n = w_in.shape
    assert o1 - (O_GLA + W_GLA - LANE) == GLA_RANK and o2 - o1 == W_SSD - (LANE - SSD_HEADS) and n - o2 == W_GATE
    tr = 256
    idx = lambda l, i: (l, i, 0)
    return pl.pallas_call(
        functools.partial(_prep_w_in_kernel, o1=o1, o2=o2),
        name="prep_w_in",
        grid=(depth, k // tr),
        in_specs=[pl.BlockSpec((None, tr, n), idx)],
        out_specs=(pl.BlockSpec((None, tr, W_MIX), idx), pl.BlockSpec((None, tr, W_GATE), idx)),
        out_shape=(jax.ShapeDtypeStruct((depth, k, W_MIX), bf16), jax.ShapeDtypeStruct((depth, k, W_GATE), bf16)),
        compiler_params=_cp(2),
    )(w_in)


def _block_diag_tiles(w):
    depth, nblk, bs, _ = w.shape
    per = LRU_TILE // bs
    w4 = w.reshape(depth, nblk // per, per, bs, bs)
    eye = jnp.eye(per, dtype=w.dtype)
    t = jnp.einsum("ltpij,pq->ltpiqj", w4, eye)
    return t.reshape(depth, nblk // per, LRU_TILE, LRU_TILE).astype(bf16)


def _rows(a):
    return a[:, None, :]


def _pad_last(a, width):
    return jnp.pad(a, [(0, 0)] * (a.ndim - 1) + [(0, width - a.shape[-1])])


def kernel(x_prompt, x_sample, mem_prompt, state_lru_h, state_lru_conv, state_gla_S, state_ssd_h, state_ssd_conv, cache_mem_k, cache_mem_v, norm_mix, w_in, lru_conv_w, lru_conv_b, lru_wa, lru_ba, lru_wx, lru_bx, lru_lambda, w_lru_out, gla_w_alpha, gla_b_alpha, gla_norm, w_gla_out, ssd_conv_w, ssd_conv_b, ssd_dt_bias, ssd_a_log, ssd_d, ssd_norm, w_ssd_out, w_mix_out, norm_xattn, norm_mem, w_xq, w_xk, w_xv, w_xo, norm_ffn, w_ffn_gate, w_ffn_up, w_ffn_down, norm_final):
    bp, lp, _ = x_prompt.shape
    bs, ls, _ = x_sample.shape
    assert ls == SUB, "sample kernels treat each 8-row group as one sequence"
    hk = GLA_HEADS * GLA_DK
    xp = x_prompt.reshape(bp * lp, D)
    xs = x_sample.reshape(bs * ls, D)
    mem = mem_prompt.reshape(bp * MEM, D)

    o1 = 2 * D + 2 * hk + 2 * D + GLA_RANK
    o2 = o1 + D + SSD_CH + SSD_HEADS
    w_mix_in, w_gate = _prep_w_in(w_in, o1, o2)
    g_mix, g_xa, g_mem, g_ffn = _rows(norm_mix), _rows(norm_xattn), _rows(norm_mem), _rows(norm_ffn)
    lru_params = (lru_conv_w, _rows(lru_conv_b), _rows(lru_lambda), _block_diag_tiles(lru_wa), _rows(lru_ba),
                  _block_diag_tiles(lru_wx), _rows(lru_bx))
    gla_params = (jnp.pad(gla_w_alpha, ((0, 0), (0, LANE - GLA_RANK), (0, 0))).astype(bf16),
                  _rows(gla_b_alpha), _rows(gla_norm))
    expand = (jnp.arange(LANE)[:, None] == (jnp.arange(D)[None, :] // SSD_P)).astype(bf16)
    ssd_params = (ssd_conv_w, _rows(ssd_conv_b), _rows(_pad_last(ssd_dt_bias, LANE)),
                  _rows(_pad_last(ssd_a_log, LANE)), _rows(jnp.repeat(ssd_d, SSD_P, axis=1)),
                  _rows(ssd_norm), expand)
    w_lo, w_go, w_so, w_mo = (w.astype(bf16) for w in (w_lru_out, w_gla_out, w_ssd_out, w_mix_out))
    wq, wo = w_xq.astype(bf16), w_xo.astype(bf16)
    wkv = jnp.concatenate([w_xk, w_xv], axis=2).astype(bf16)
    wfg, wfu, wfd = w_ffn_gate.astype(bf16), w_ffn_up.astype(bf16), w_ffn_down.astype(bf16)
    pad5 = ((0, 0), (0, 0), (SUB - (CONV_W - 1), 0), (0, 0))
    bp_lru = jnp.pad(state_lru_conv, pad5).reshape(DEPTH, bs * ls, D)
    bp_ssd = jnp.pad(state_ssd_conv, pad5).reshape(DEPTH, bs * ls, SSD_CH)
    h0_lru = state_lru_h[:, :, None, :]

    p_out = [[] for _ in range(7)]
    s_out = [[] for _ in range(5)]
    for l in range(DEPTH):
        kv = _norm_matmul("mem_kv", mem, g_mem, wkv, l, T_ROW)
        yl, yg, ys, p_h, p_tail, p_s, p_sh, p_stail = _mix_prompt(
            xp, g_mix, w_mix_in, lru_params, gla_params, ssd_params, l, bp, lp, T_MIX)
        xp = _merge("merge_prompt", xp, g_mix, w_gate, yl, yg, ys, w_lo, w_go, w_so, w_mo, l, T_MERGE)
        xp = _xattn_prompt(xp, g_xa, wq, kv.reshape(bp, MEM, 2 * D), wo, l, bp, lp, T_XATTN)
        xp = _ffn("ffn_prompt", xp, g_ffn, wfg, wfu, wfd, l, T_FFN)
        p_vals = (p_h[:, 0], p_tail[:, SUB - 3:], p_s, p_sh, p_stail[:, SUB - 3:],
                  kv[:, :D].reshape(bp, MEM, XA_HEADS, XA_HD), kv[:, D:].reshape(bp, MEM, XA_HEADS, XA_HD))
        for lst, val in zip(p_out, p_vals):
            lst.append(val)

        u = _norm_matmul("in_proj_sample", xs, g_mix, w_mix_in, l, T_SAMPLE)
        yl, h_last = _lru_sample(u, h0_lru, bp_lru[l], lru_params, l, T_SAMPLE)
        yg, s_s = _gla_sample(u, state_gla_S, gla_params, l, NB_STATE, ls)
        ys, s_sh = _ssd_sample(u, bp_ssd[l], state_ssd_h, ssd_params, l, NB_STATE, ls)
        xs = _merge("merge_sample", xs, g_mix, w_gate, yl, yg, ys, w_lo, w_go, w_so, w_mo, l, T_SAMPLE)
        xs = _xattn_sample(xs, g_xa, wq, cache_mem_k, cache_mem_v, wo, l, NB_STATE, ls)
        xs = _ffn("ffn_sample", xs, g_ffn, wfg, wfu, wfd, l, T_SAMPLE)
        u3 = u.reshape(bs, ls, W_MIX)
        s_vals = (h_last[:, 0], u3[:, ls - 3:, :D], s_s, s_sh,
                  u3[:, ls - 3:, O_SSD + D:O_SSD + D + SSD_CH])
        for lst, val in zip(s_out, s_vals):
            lst.append(val)

    y_prompt = _final_norm("final_norm_prompt", xp, norm_final.reshape(1, D), T_ROW).reshape(bp, lp, D)
    y_sample = _final_norm("final_norm_sample", xs, norm_final.reshape(1, D), T_SAMPLE).reshape(bs, ls, D)
    p_stack = [jnp.stack(t, axis=0) for t in p_out]
    s_stack = [jnp.stack(t, axis=0) for t in s_out]
    return (y_prompt, y_sample, *p_stack, *s_stack)
```

```python
import functools

import jax
import jax.numpy as jnp
from jax import lax
from jax.experimental import pallas as pl
from jax.experimental.pallas import tpu as pltpu

f32 = jnp.float32
bf16 = jnp.bfloat16

D = 1024
DEPTH = 4
EPS = 1e-6
CONV_W = 4
LRU_C = 8.0
LRU_TILE = 256
GLA_HEADS, GLA_DK, GLA_DV, GLA_RANK, GLA_TAU, GLA_CHUNK = 4, 128, 256, 16, 16.0, 64
SSD_HEADS, SSD_P, SSD_GROUPS, SSD_N, SSD_CHUNK = 16, 64, 2, 64, 128
SSD_HP = SSD_HEADS // SSD_GROUPS * SSD_P
SSD_CH = D + 2 * SSD_GROUPS * SSD_N
XA_HEADS, XA_HD, MEM = 4, 256, 256
D_FF = 2816
LANE = 128
SUB = 8
W_LRU, W_GLA, W_SSD, W_GATE = 2 * D, 3 * D + LANE, D + SSD_CH + LANE, 3 * D
W_MIX = W_LRU + W_GLA + W_SSD
O_GLA, O_SSD = W_LRU, W_LRU + W_GLA
VMEM_LIMIT = 56 * 1024 * 1024

T_MIX = 256
T_ROW = 512
T_XATTN = 1024
T_FFN = 1024
T_MERGE = 512
T_SAMPLE = 256
NB_STATE = 8


def _cp(n_axes):
    return pltpu.CompilerParams(dimension_semantics=("arbitrary",) * n_axes,
                                vmem_limit_bytes=VMEM_LIMIT)


def _const_spec(shape):
    nd = len(shape)
    return pl.BlockSpec(shape, lambda *_: (0,) * nd, pipeline_mode=pl.Buffered(1))


def _layer_spec(shape, layer):
    nd = len(shape)
    return pl.BlockSpec((None,) + tuple(shape), lambda *_: (layer,) + (0,) * nd,
                        pipeline_mode=pl.Buffered(1))


def _rms(x, g):
    return x * lax.rsqrt(jnp.mean(x * x, axis=-1, keepdims=True) + EPS) * g


def _dot(a, b):
    return jnp.dot(a.astype(bf16), b.astype(bf16), preferred_element_type=f32)


def _dot_nt(a, b):
    return lax.dot_general(a.astype(bf16), b.astype(bf16), (((1,), (1,)), ((), ())),
                           preferred_element_type=f32)


def _split(x):
    hi = x.astype(bf16)
    return hi, (x - hi.astype(f32)).astype(bf16)


def _dot_split(m, x):
    hi, lo = _split(x)
    return (jnp.dot(m, hi, preferred_element_type=f32) + jnp.dot(m, lo, preferred_element_type=f32))


def _dot_split_rows(x, m):
    hi, lo = _split(x)
    return jnp.dot(hi, m, preferred_element_type=f32) + jnp.dot(lo, m, preferred_element_type=f32)


def _bdot(a, b, nb, rows, mode):
    a = a.astype(bf16)
    b = b.astype(bf16)
    if mode == "tn":
        if nb == 1:
            return lax.dot_general(a, b, (((0,), (0,)), ((), ())), preferred_element_type=f32)[None]
        a3 = a.reshape(nb, rows, a.shape[-1])
        b3 = b.reshape(nb, rows, b.shape[-1])
        return jnp.einsum("bki,bkj->bij", a3, b3, preferred_element_type=f32)
    if nb == 1:
        if mode == "nn":
            return jnp.dot(a, b[0], preferred_element_type=f32)
        return lax.dot_general(a, b[0], (((1,), (1,)), ((), ())), preferred_element_type=f32)
    a3 = a.reshape(nb, rows, a.shape[-1])
    eq = "bik,bkj->bij" if mode == "nn" else "bik,bjk->bij"
    out = jnp.einsum(eq, a3, b, preferred_element_type=f32)
    return out.reshape(nb * rows, out.shape[-1])


def _seq_masks(r, c):
    ri = lax.broadcasted_iota(jnp.int32, (r, r), 0)
    ci = lax.broadcasted_iota(jnp.int32, (r, r), 1)
    if r == c:
        same = jnp.full((r, r), True)
    else:
        shift = c.bit_length() - 1
        same = (ri >> shift) == (ci >> shift)
    causal = same & (ri >= ci)
    return causal, same


def _conv_carry(x, tail, w_ref, b_ref):
    xe = jnp.concatenate([tail, x], axis=0)
    y = b_ref[...] + w_ref[0:1, :] * pltpu.roll(xe, 3, 0)[SUB:]
    y = y + w_ref[1:2, :] * pltpu.roll(xe, 2, 0)[SUB:]
    y = y + w_ref[2:3, :] * pltpu.roll(xe, 1, 0)[SUB:]
    return y + w_ref[3:4, :] * x


def _conv_groups(x, bufpad, w_ref, b_ref):
    rows = x.shape[0]
    l = lax.broadcasted_iota(jnp.int32, (rows, 1), 0) & (SUB - 1)
    y = b_ref[...]
    for k in range(CONV_W - 1):
        s = CONV_W - 1 - k
        sh = jnp.where(l >= s, pltpu.roll(x, s, 0), pltpu.roll(bufpad, rows - SUB + s, 0))
        y = y + w_ref[k:k + 1, :] * sh
    return y + w_ref[3:4, :] * x


def _in_proj_sample_kernel(x_ref, g_ref, w_ref, ul_ref, ug_ref, us_ref):
    h = _rms(x_ref[...], g_ref[...]).astype(bf16)
    ul_ref[...] = jnp.dot(h, w_ref[:, 0:W_LRU], preferred_element_type=f32)
    ug_ref[...] = jnp.dot(h, w_ref[:, O_GLA:O_GLA + W_GLA], preferred_element_type=f32)
    us_ref[...] = jnp.dot(h, w_ref[:, O_SSD:W_MIX], preferred_element_type=f32)


def _in_proj_sample(x, g, w, layer, tm):
    m, k = x.shape
    row = lambda i: (i, 0)
    return pl.pallas_call(
        _in_proj_sample_kernel,
        name="in_proj_sample",
        out_shape=tuple(jax.ShapeDtypeStruct((m, n), f32) for n in (W_LRU, W_GLA, W_SSD)),
        grid=(m // tm,),
        in_specs=[pl.BlockSpec((tm, k), row), _layer_spec((1, k), layer), _layer_spec((k, W_MIX), layer)],
        out_specs=tuple(pl.BlockSpec((tm, n), row) for n in (W_LRU, W_GLA, W_SSD)),
        compiler_params=_cp(1),
    )(x, g, w)


def _mem_kv_kernel(x_ref, g_ref, w_ref, kv_ref, k4_ref, v4_ref):
    kv = _dot(_rms(x_ref[...], g_ref[...]), w_ref[...])
    kv_ref[...] = kv
    rows = kv.shape[0]
    k4_ref[...] = kv[:, :D].reshape(rows, XA_HEADS, XA_HD)
    v4_ref[...] = kv[:, D:].reshape(rows, XA_HEADS, XA_HD)


def _mem_kv(mem, g, wkv, layer, tm):
    m, k = mem.shape
    row = lambda i: (i, 0)
    row3 = lambda i: (i, 0, 0)
    return pl.pallas_call(
        _mem_kv_kernel,
        name="mem_kv",
        out_shape=(jax.ShapeDtypeStruct((m, 2 * D), f32), jax.ShapeDtypeStruct((m, XA_HEADS, XA_HD), f32),
                   jax.ShapeDtypeStruct((m, XA_HEADS, XA_HD), f32)),
        grid=(m // tm,),
        in_specs=[pl.BlockSpec((tm, k), row), _layer_spec((1, k), layer), _layer_spec((k, 2 * D), layer)],
        out_specs=(pl.BlockSpec((tm, 2 * D), row), pl.BlockSpec((tm, XA_HEADS, XA_HD), row3),
                   pl.BlockSpec((tm, XA_HEADS, XA_HD), row3)),
        compiler_params=_cp(1),
    )(mem, g, wkv)


def _final_norm_kernel(x_ref, g_ref, o_ref):
    o_ref[...] = _rms(x_ref[...], g_ref[...])


def _final_norm(name, x, g, tm):
    m, k = x.shape
    return pl.pallas_call(
        _final_norm_kernel,
        name=name,
        out_shape=jax.ShapeDtypeStruct((m, k), f32),
        grid=(m // tm,),
        in_specs=[pl.BlockSpec((tm, k), lambda i: (i, 0)), _const_spec((1, k))],
        out_specs=pl.BlockSpec((tm, k), lambda i: (i, 0)),
        compiler_params=_cp(1),
    )(x, g)


def _lru_decay_and_scale(log_a):
    a = jnp.exp(log_a)
    m = 1.0 - a * a
    return a, jnp.where(m > 0.0, m * lax.rsqrt(m), 0.0)


def _lru_math(xc, lam_ref, wa_ref, ba_ref, wx_ref, bx_ref):
    xcb = xc.astype(bf16)
    nt = D // LRU_TILE
    r_pre = jnp.concatenate([jnp.dot(xcb[:, i * LRU_TILE:(i + 1) * LRU_TILE], wa_ref[i],
                                     preferred_element_type=f32) for i in range(nt)], axis=1)
    i_pre = jnp.concatenate([jnp.dot(xcb[:, i * LRU_TILE:(i + 1) * LRU_TILE], wx_ref[i],
                                     preferred_element_type=f32) for i in range(nt)], axis=1)
    r = jax.nn.sigmoid(r_pre + ba_ref[...])
    ig = jax.nn.sigmoid(i_pre + bx_ref[...])
    a, mult = _lru_decay_and_scale(-LRU_C * r * jax.nn.softplus(-lam_ref[...]))
    return a, mult * (ig * xc)


def _group_scan(a, b):
    l = lax.broadcasted_iota(jnp.int32, (a.shape[0], 1), 0) & (SUB - 1)
    for s in (1, 2, 4):
        m = l >= s
        b = jnp.where(m, a * pltpu.roll(b, s, 0) + b, b)
        a = jnp.where(m, a * pltpu.roll(a, s, 0), a)
    return a, b


def _lru_prompt_tiles(hn, w_ref, p, yl_ref, lh_ref, ltail_ref, hc_scr, ltail_scr):
    cw_ref, cb_ref, lam_ref, wa_ref, ba_ref, wx_ref, bx_ref = p
    rows = hn.shape[0]
    for ci in range(D // LRU_TILE):
        cs = slice(ci * LRU_TILE, (ci + 1) * LRU_TILE)
        x = jnp.dot(hn, w_ref[:, cs], preferred_element_type=f32)
        gate = jnp.dot(hn, w_ref[:, D + ci * LRU_TILE:D + (ci + 1) * LRU_TILE], preferred_element_type=f32)
        xe = jnp.concatenate([ltail_scr[:, cs], x], axis=0)
        xc = cb_ref[:, cs] + cw_ref[0:1, cs] * pltpu.roll(xe, 3, 0)[SUB:]
        xc = xc + cw_ref[1:2, cs] * pltpu.roll(xe, 2, 0)[SUB:]
        xc = xc + cw_ref[2:3, cs] * pltpu.roll(xe, 1, 0)[SUB:]
        xc = xc + cw_ref[3:4, cs] * x
        xcb = xc.astype(bf16)
        r = jax.nn.sigmoid(jnp.dot(xcb, wa_ref[ci], preferred_element_type=f32) + ba_ref[:, cs])
        ig = jax.nn.sigmoid(jnp.dot(xcb, wx_ref[ci], preferred_element_type=f32) + bx_ref[:, cs])
        a, mult = _lru_decay_and_scale(-LRU_C * r * jax.nn.softplus(-lam_ref[:, cs]))
        b = mult * (ig * xc)
        a, b = _group_scan(a, b)
        carry = hc_scr[0:1, cs]
        hs = []
        for j in range(rows // SUB):
            hj = a[j * SUB:(j + 1) * SUB] * carry + b[j * SUB:(j + 1) * SUB]
            carry = hj[SUB - 1:SUB]
            hs.append(hj)
        yl_ref[:, cs] = jnp.concatenate(hs, axis=0) * jax.nn.gelu(gate)
        hc = jnp.broadcast_to(carry, (SUB, LRU_TILE))
        hc_scr[:, cs] = hc
        lh_ref[0, :, cs] = hc
        ltail = x[rows - SUB:]
        ltail_scr[:, cs] = ltail
        ltail_ref[0, :, cs] = ltail


def _lru_sample_kernel(u_ref, h0_ref, bp_ref, cw_ref, cb_ref, lam_ref, wa_ref, ba_ref, wx_ref, bx_ref,
                       y_ref, h_ref):
    x = u_ref[:, 0:D]
    gate = u_ref[:, D:W_LRU]
    xc = _conv_groups(x, bp_ref[...], cw_ref, cb_ref)
    a, b = _lru_math(xc, lam_ref, wa_ref, ba_ref, wx_ref, bx_ref)
    a, b = _group_scan(a, b)
    nseq = x.shape[0] // SUB
    h3 = a.reshape(nseq, SUB, D) * h0_ref[...] + b.reshape(nseq, SUB, D)
    h_ref[...] = h3[:, SUB - 1:SUB, :]
    y_ref[...] = h3.reshape(nseq * SUB, D) * jax.nn.gelu(gate)


def _lru_param_specs(layer):
    nt = D // LRU_TILE
    return [_layer_spec((CONV_W, D), layer), _layer_spec((1, D), layer), _layer_spec((1, D), layer),
            _layer_spec((nt, LRU_TILE, LRU_TILE), layer), _layer_spec((1, D), layer),
            _layer_spec((nt, LRU_TILE, LRU_TILE), layer), _layer_spec((1, D), layer)]


def _lru_sample(u, h0, bufpad, params, layer, tm):
    m = u.shape[0]
    row = lambda i: (i, 0)
    seq3 = lambda i: (i, 0, 0)
    return pl.pallas_call(
        _lru_sample_kernel,
        name="lru_sample",
        out_shape=(jax.ShapeDtypeStruct((m, D), f32), jax.ShapeDtypeStruct((m // SUB, 1, D), f32)),
        grid=(m // tm,),
        in_specs=[pl.BlockSpec((tm, W_LRU), row),
                  pl.BlockSpec((None, tm // SUB, 1, D), lambda i: (layer, i, 0, 0)),
                  pl.BlockSpec((tm, D), row)] + _lru_param_specs(layer),
        out_specs=(pl.BlockSpec((tm, D), row), pl.BlockSpec((tm // SUB, 1, D), seq3)),
        compiler_params=_cp(1),
    )(u, h0, bufpad, *params)


def _gla_core(u, s_list, nb, c, p, chain=False):
    wal_ref, bal_ref, gn_ref = p
    r_rows = nb * c
    hk = GLA_HEADS * GLA_DK
    q = u[:, 0:hk]
    k = u[:, hk:2 * hk]
    v = u[:, 2 * hk:2 * hk + D]
    rg = u[:, 2 * hk + D:2 * hk + 2 * D]
    glow = u[:, 2 * hk + 2 * D:]
    z = _dot(glow, wal_ref[...]) + bal_ref[...]
    g = jax.nn.log_sigmoid(z) / GLA_TAU
    causal, same = _seq_masks(r_rows, c)
    g_hi, g_lo = _split(g)
    tril = causal.astype(bf16)
    ones_sq = same.astype(bf16)
    bcum = jnp.dot(tril, g_hi, preferred_element_type=f32) + jnp.dot(tril, g_lo, preferred_element_type=f32)
    btot = jnp.dot(ones_sq, g_hi, preferred_element_type=f32) + jnp.dot(ones_sq, g_lo, preferred_element_type=f32)
    qt = (q * GLA_DK ** -0.5) * jnp.exp(bcum)
    kt = k * jnp.exp(-bcum)
    ke = k * jnp.exp(btot - bcum)
    ones_v = jnp.ones((r_rows, GLA_DV), bf16)
    if chain:
        assert nb <= SUB
        tot8 = jnp.concatenate([btot[j * c:j * c + 1, :] for j in range(nb)] + [btot[0:SUB - nb, :]], axis=0)
        dec_t = jnp.exp(tot8.T)
    outs, new_s = [], []
    for h in range(GLA_HEADS):
        ks = slice(h * GLA_DK, (h + 1) * GLA_DK)
        vs = slice(h * GLA_DV, (h + 1) * GLA_DV)
        s_h = s_list[h]
        att = jnp.where(causal, _dot_nt(qt[:, ks], kt[:, ks]), 0.0)
        upd = _bdot(ke[:, ks], v[:, vs], nb, c, "tn")
        if chain:
            dec = [dec_t[h * GLA_DK:(h + 1) * GLA_DK, j:j + 1] for j in range(nb)]
        else:
            blast = _bdot(g_hi[:, ks], ones_v, nb, c, "tn") + _bdot(g_lo[:, ks], ones_v, nb, c, "tn")
            dec = jnp.exp(blast)
        if chain:
            cur = s_h[0]
            starts = []
            for j in range(nb):
                starts.append(cur)
                cur = dec[j] * cur + upd[j]
            s_in = jnp.stack(starts, axis=0)
            new_s.append(cur[None])
        else:
            s_in = s_h
            new_s.append(dec * s_h + upd)
        o_h = _dot(att, v[:, vs]) + _bdot(qt[:, ks], s_in, nb, c, "nn")
        o_h = o_h * lax.rsqrt(jnp.mean(o_h * o_h, axis=-1, keepdims=True) + EPS) * gn_ref[...]
        outs.append(o_h)
    o = jnp.concatenate(outs, axis=1)
    return o * jax.nn.silu(rg), new_s


def _gla_sample_kernel(u_ref, s0_ref, wal_ref, bal_ref, gn_ref, o_ref, s_ref, *, nb, c):
    s_list = [s0_ref[:, h] for h in range(GLA_HEADS)]
    o, new_s = _gla_core(u_ref[...], s_list, nb, c, (wal_ref, bal_ref, gn_ref))
    o_ref[...] = o
    for h in range(GLA_HEADS):
        s_ref[:, h] = new_s[h]


def _gla_param_specs(layer):
    hk = GLA_HEADS * GLA_DK
    return [_layer_spec((LANE, hk), layer), _layer_spec((1, hk), layer), _layer_spec((1, GLA_DV), layer)]


def _gla_sample(u, s0_all, params, layer, nb, c):
    nseq = s0_all.shape[1]
    blk = (nb, GLA_HEADS, GLA_DK, GLA_DV)
    return pl.pallas_call(
        functools.partial(_gla_sample_kernel, nb=nb, c=c),
        name="gla_sample",
        out_shape=(jax.ShapeDtypeStruct((nseq * c, D), f32), jax.ShapeDtypeStruct(s0_all.shape[1:], f32)),
        grid=(nseq // nb,),
        in_specs=[pl.BlockSpec((nb * c, W_GLA), lambda i: (i, 0)),
                  pl.BlockSpec((None,) + blk, lambda i: (layer, i, 0, 0, 0))] + _gla_param_specs(layer),
        out_specs=(pl.BlockSpec((nb * c, D), lambda i: (i, 0)), pl.BlockSpec(blk, lambda i: (i, 0, 0, 0))),
        compiler_params=_cp(1),
    )(u, s0_all, *params)


def _ssd_core(sz, xbc, sdt, h_list, nb, c, p):
    dtb_ref, alog_ref, dskip_ref, nrm_ref, exp_ref = p
    r_rows = nb * c
    sx = xbc[:, :D]
    dt = jax.nn.softplus(sdt + dtb_ref[...])
    da = dt * (-jnp.exp(alog_ref[...]))
    causal, same = _seq_masks(r_rows, c)
    cum = _dot_split(causal.astype(bf16), da)
    tot = _dot_split(same.astype(bf16), da)
    ecum = jnp.exp(cum)
    wgt = jnp.exp(tot - cum) * dt
    expd = _dot_split_rows(jnp.concatenate([ecum, wgt, da], axis=0), exp_ref[...])
    ecum_x = expd[0:r_rows]
    wgt_x = expd[r_rows:2 * r_rows]
    da_x = expd[2 * r_rows:]
    cum_t = cum.T
    dt_t = dt.T
    lane = lax.broadcasted_iota(jnp.int32, (r_rows, LANE), 1)
    xw = sx * wgt_x
    da_hi, da_lo = _split(da_x)
    ones_n = jnp.ones((r_rows, SSD_N), bf16)
    ys, new_h = [], []
    for grp in range(SSD_GROUPS):
        bm = xbc[:, D + grp * SSD_N:D + (grp + 1) * SSD_N]
        cm = xbc[:, D + (SSD_GROUPS + grp) * SSD_N:D + (SSD_GROUPS + grp + 1) * SSD_N]
        cb = _dot_nt(cm, bm)
        gs = slice(grp * SSD_HP, (grp + 1) * SSD_HP)
        y_inter = _bdot(cm, h_list[grp], nb, c, "nt") * ecum_x[:, gs]
        pairs = []
        for pr in range(SSD_HP // LANE):
            col = grp * SSD_HP + pr * LANE
            xp = sx[:, col:col + LANE]
            acc = None
            for half in range(2):
                hd = (col // SSD_P) + half
                seg = cum[:, hd:hd + 1] - cum_t[hd:hd + 1, :]
                m = cb * jnp.exp(jnp.where(causal, seg, -jnp.inf)) * dt_t[hd:hd + 1, :]
                xh = jnp.where((lane < SSD_P) if half == 0 else (lane >= SSD_P), xp, 0.0)
                part = _dot(m, xh)
                acc = part if acc is None else acc + part
            pairs.append(acc)
        ys.append(jnp.concatenate(pairs, axis=1) + y_inter)
        dec = jnp.exp(_bdot(da_hi[:, gs], ones_n, nb, c, "tn") + _bdot(da_lo[:, gs], ones_n, nb, c, "tn"))
        new_h.append(dec * h_list[grp] + _bdot(xw[:, gs], bm, nb, c, "tn"))
    y = jnp.concatenate(ys, axis=1) + dskip_ref[...] * sx
    y = y * jax.nn.silu(sz)
    halves = []
    for grp in range(SSD_GROUPS):
        yh = y[:, grp * SSD_HP:(grp + 1) * SSD_HP]
        halves.append(yh * lax.rsqrt(jnp.mean(yh * yh, axis=-1, keepdims=True) + EPS))
    return jnp.concatenate(halves, axis=1) * nrm_ref[...], new_h


def _ssd_sample_kernel(u_ref, bp_ref, h0_ref, cw_ref, cb_ref, dtb_ref, alog_ref, dskip_ref, nrm_ref, exp_ref,
                       y_ref, h_ref, *, nb, c):
    sz = u_ref[:, 0:D]
    raw = u_ref[:, D:D + SSD_CH]
    sdt = u_ref[:, D + SSD_CH:W_SSD]
    xbc = jax.nn.silu(_conv_groups(raw, bp_ref[...], cw_ref, cb_ref))
    hpg = SSD_HEADS // SSD_GROUPS
    h_list = [h0_ref[:, grp * hpg:(grp + 1) * hpg].reshape(nb, SSD_HP, SSD_N) for grp in range(SSD_GROUPS)]
    y, new_h = _ssd_core(sz, xbc, sdt, h_list, nb, c, (dtb_ref, alog_ref, dskip_ref, nrm_ref, exp_ref))
    y_ref[...] = y
    for grp in range(SSD_GROUPS):
        h_ref[:, grp * hpg:(grp + 1) * hpg] = new_h[grp].reshape(nb, hpg, SSD_P, SSD_N)


def _ssd_param_specs(layer):
    return [_layer_spec((CONV_W, SSD_CH), layer), _layer_spec((1, SSD_CH), layer), _layer_spec((1, LANE), layer),
            _layer_spec((1, LANE), layer), _layer_spec((1, D), layer), _layer_spec((1, D), layer),
            _const_spec((LANE, D))]


def _ssd_sample(u, bufpad, h0_all, params, layer, nb, c):
    nseq = h0_all.shape[1]
    blk = (nb, SSD_HEADS, SSD_P, SSD_N)
    row = lambda i: (i, 0)
    return pl.pallas_call(
        functools.partial(_ssd_sample_kernel, nb=nb, c=c),
        name="ssd_sample",
        out_shape=(jax.ShapeDtypeStruct((nseq * c, D), f32), jax.ShapeDtypeStruct(h0_all.shape[1:], f32)),
        grid=(nseq // nb,),
        in_specs=[pl.BlockSpec((nb * c, W_SSD), row), pl.BlockSpec((nb * c, SSD_CH), row),
                  pl.BlockSpec((None,) + blk, lambda i: (layer, i, 0, 0, 0))] + _ssd_param_specs(layer),
        out_specs=(pl.BlockSpec((nb * c, D), row), pl.BlockSpec(blk, lambda i: (i, 0, 0, 0))),
        compiler_params=_cp(1),
    )(u, bufpad, h0_all, *params)


def _mix_prompt_kernel(x_ref, g_ref, w_ref, *refs):
    lru_p = refs[0:7]
    gla_p = refs[7:10]
    ssd_cw, ssd_cb = refs[10:12]
    ssd_p = refs[12:17]
    yl_ref, yg_ref, ys_ref, lh_ref, ltail_ref, gs_ref, sh_ref, stail_ref = refs[17:25]
    hc_scr, ltail_scr, s_scr, h_scr, stail_scr = refs[25:30]
    t = pl.program_id(1)

    @pl.when(t == 0)
    def _():
        hc_scr[...] = jnp.zeros_like(hc_scr)
        ltail_scr[...] = jnp.zeros_like(ltail_scr)
        s_scr[...] = jnp.zeros_like(s_scr)
        h_scr[...] = jnp.zeros_like(h_scr)
        stail_scr[...] = jnp.zeros_like(stail_scr)

    hn = _rms(x_ref[...], g_ref[...]).astype(bf16)
    rows = hn.shape[0]

    _lru_prompt_tiles(hn, w_ref, lru_p, yl_ref, lh_ref, ltail_ref, hc_scr, ltail_scr)

    u = jnp.dot(hn, w_ref[:, O_GLA:O_GLA + W_GLA], preferred_element_type=f32)
    s_list = [s_scr[h][None] for h in range(GLA_HEADS)]
    o, s_list = _gla_core(u, s_list, rows // GLA_CHUNK, GLA_CHUNK, gla_p, chain=True)
    yg_ref[...] = o
    for h in range(GLA_HEADS):
        s_scr[h] = s_list[h][0]

    u = jnp.dot(hn, w_ref[:, O_SSD:O_SSD + W_SSD], preferred_element_type=f32)
    sz = u[:, :D]
    raw = u[:, D:D + SSD_CH]
    sdt = u[:, D + SSD_CH:]
    xbc = jax.nn.silu(_conv_carry(raw, stail_scr[...], ssd_cw, ssd_cb))
    h_list = [h_scr[grp][None] for grp in range(SSD_GROUPS)]
    c = SSD_CHUNK
    for j in range(rows // c):
        rs = slice(j * c, (j + 1) * c)
        y, h_list = _ssd_core(sz[rs], xbc[rs], sdt[rs], h_list, 1, c, ssd_p)
        ys_ref[rs, :] = y
    for grp in range(SSD_GROUPS):
        h_scr[grp] = h_list[grp][0]
    stail = raw[rows - SUB:]
    stail_scr[...] = stail
    stail_ref[0] = stail

    @pl.when(t == pl.num_programs(1) - 1)
    def _():
        gs_ref[0] = s_scr[...]
        hpg = SSD_HEADS // SSD_GROUPS
        for grp in range(SSD_GROUPS):
            sh_ref[0, grp * hpg:(grp + 1) * hpg] = h_scr[grp].reshape(hpg, SSD_P, SSD_N)


def _mix_prompt(x, g, w, lru_params, gla_params, ssd_params, layer, nseq, seqlen, tt):
    nt = seqlen // tt
    row = lambda b, t: (b * nt + t, 0)
    seq3 = lambda b, t: (b, 0, 0)
    seq4 = lambda b, t: (b, 0, 0, 0)
    rows = nseq * seqlen
    return pl.pallas_call(
        _mix_prompt_kernel,
        name="mix_prompt",
        out_shape=(jax.ShapeDtypeStruct((rows, D), f32), jax.ShapeDtypeStruct((rows, D), f32),
                   jax.ShapeDtypeStruct((rows, D), f32),
                   jax.ShapeDtypeStruct((nseq, SUB, D), f32), jax.ShapeDtypeStruct((nseq, SUB, D), f32),
                   jax.ShapeDtypeStruct((nseq, GLA_HEADS, GLA_DK, GLA_DV), f32),
                   jax.ShapeDtypeStruct((nseq, SSD_HEADS, SSD_P, SSD_N), f32),
                   jax.ShapeDtypeStruct((nseq, SUB, SSD_CH), f32)),
        grid=(nseq, nt),
        in_specs=[pl.BlockSpec((tt, D), row), _layer_spec((1, D), layer), _layer_spec((D, W_MIX), layer)]
        + _lru_param_specs(layer) + _gla_param_specs(layer) + _ssd_param_specs(layer),
        out_specs=(pl.BlockSpec((tt, D), row), pl.BlockSpec((tt, D), row), pl.BlockSpec((tt, D), row),
                   pl.BlockSpec((1, SUB, D), seq3), pl.BlockSpec((1, SUB, D), seq3),
                   pl.BlockSpec((1, GLA_HEADS, GLA_DK, GLA_DV), seq4),
                   pl.BlockSpec((1, SSD_HEADS, SSD_P, SSD_N), seq4),
                   pl.BlockSpec((1, SUB, SSD_CH), seq3)),
        scratch_shapes=[pltpu.VMEM((SUB, D), f32), pltpu.VMEM((SUB, D), f32),
                        pltpu.VMEM((GLA_HEADS, GLA_DK, GLA_DV), f32),
                        pltpu.VMEM((SSD_GROUPS, SSD_HP, SSD_N), f32), pltpu.VMEM((SUB, SSD_CH), f32)],
        compiler_params=_cp(2),
    )(x, g, w, *lru_params, *gla_params, *ssd_params)


def _merge_kernel(x_ref, g_ref, wg_ref, yl_ref, yg_ref, ys_ref, wl_ref, wgl_ref, ws_ref, wm_ref, o_ref):
    x = x_ref[...]
    gates = jax.nn.sigmoid(_dot(_rms(x, g_ref[...]), wg_ref[...]))
    merged = gates[:, 0:D] * _dot(yl_ref[...], wl_ref[...])
    merged = merged + gates[:, D:2 * D] * _dot(yg_ref[...], wgl_ref[...])
    merged = merged + gates[:, 2 * D:] * _dot(ys_ref[...], ws_ref[...])
    o_ref[...] = x + _dot(merged, wm_ref[...])


def _merge(name, x, g, wgate, yl, yg, ys, wl, wgl, ws, wm, layer, tm):
    m = x.shape[0]
    row = pl.BlockSpec((tm, D), lambda i: (i, 0))
    wsq = _layer_spec((D, D), layer)
    return pl.pallas_call(
        _merge_kernel,
        name=name,
        out_shape=jax.ShapeDtypeStruct((m, D), f32),
        grid=(m // tm,),
        in_specs=[row, _layer_spec((1, D), layer), _layer_spec((D, W_GATE), layer), row, row, row,
                  wsq, wsq, wsq, wsq],
        out_specs=row,
        compiler_params=_cp(1),
    )(x, g, wgate, yl, yg, ys, wl, wgl, ws, wm)


def _attend(q, get_k, get_v, nb, rows):
    outs = []
    for h in range(XA_HEADS):
        hs = slice(h * XA_HD, (h + 1) * XA_HD)
        s = _bdot(q[:, hs], get_k(h), nb, rows, "nt") * XA_HD ** -0.5
        s = s - jnp.max(s, axis=-1, keepdims=True)
        e = jnp.exp(s)
        p = e / jnp.sum(e, axis=-1, keepdims=True)
        outs.append(_bdot(p, get_v(h), nb, rows, "nn"))
    return jnp.concatenate(outs, axis=1)


def _xattn_prompt_kernel(x_ref, g_ref, wq_ref, k_ref, v_ref, wo_ref, o_ref):
    x = x_ref[...]
    q = _dot(_rms(x, g_ref[...]), wq_ref[...])
    o = _attend(q, lambda h: k_ref[:, :, h * XA_HD:(h + 1) * XA_HD],
                lambda h: v_ref[:, :, h * XA_HD:(h + 1) * XA_HD], 1, x.shape[0])
    o_ref[...] = x + _dot(o, wo_ref[...])


def _xattn_sample_kernel(x_ref, g_ref, wq_ref, k_ref, v_ref, wo_ref, o_ref, *, nb, rows):
    x = x_ref[...]
    q = _dot(_rms(x, g_ref[...]), wq_ref[...]).reshape(nb, rows, D)
    qh = jnp.concatenate([q[:, :, h * XA_HD:(h + 1) * XA_HD] for h in range(XA_HEADS)], axis=1)
    k2 = k_ref[...].reshape(nb, MEM * XA_HEADS, XA_HD).astype(bf16)
    v2 = v_ref[...].reshape(nb, MEM * XA_HEADS, XA_HD).astype(bf16)
    s = jnp.einsum("bik,bjk->bij", qh.astype(bf16), k2, preferred_element_type=f32) * XA_HD ** -0.5
    shp = (XA_HEADS * rows, MEM * XA_HEADS)
    q_head = lax.broadcasted_iota(jnp.int32, shp, 0) // rows
    m_head = lax.broadcasted_iota(jnp.int32, shp, 1) & (XA_HEADS - 1)
    s = jnp.where((q_head == m_head)[None], s, -jnp.inf)
    s = s - jnp.max(s, axis=-1, keepdims=True)
    e = jnp.exp(s)
    p = e / jnp.sum(e, axis=-1, keepdims=True)
    o2 = jnp.einsum("bij,bjk->bik", p.astype(bf16), v2, preferred_element_type=f32)
    o = jnp.concatenate([o2[:, h * rows:(h + 1) * rows, :] for h in range(XA_HEADS)], axis=2)
    o_ref[...] = x + _dot(o.reshape(nb * rows, D), wo_ref[...])


def _xattn_prompt(x, g, wq, kv, wo, layer, nseq, seqlen, tq):
    nt = seqlen // tq
    row = pl.BlockSpec((tq, D), lambda b, t: (b * nt + t, 0))
    return pl.pallas_call(
        _xattn_prompt_kernel,
        name="xattn_prompt",
        out_shape=jax.ShapeDtypeStruct(x.shape, f32),
        grid=(nseq, nt),
        in_specs=[row, _layer_spec((1, D), layer), _layer_spec((D, D), layer),
                  pl.BlockSpec((1, MEM, D), lambda b, t: (b, 0, 0)),
                  pl.BlockSpec((1, MEM, D), lambda b, t: (b, 0, 1)), _layer_spec((D, D), layer)],
        out_specs=row,
        compiler_params=_cp(2),
    )(x, g, wq, kv, kv, wo)


def _xattn_sample(x, g, wq, k_all, v_all, wo, layer, nb, rows):
    nseq = k_all.shape[1]
    row = pl.BlockSpec((nb * rows, D), lambda i: (i, 0))
    kvspec = pl.BlockSpec((None, nb, MEM, XA_HEADS, XA_HD), lambda i: (layer, i, 0, 0, 0))
    return pl.pallas_call(
        functools.partial(_xattn_sample_kernel, nb=nb, rows=rows),
        name="xattn_sample",
        out_shape=jax.ShapeDtypeStruct(x.shape, f32),
        grid=(nseq // nb,),
        in_specs=[row, _layer_spec((1, D), layer), _layer_spec((D, D), layer), kvspec, kvspec,
                  _layer_spec((D, D), layer)],
        out_specs=row,
        compiler_params=_cp(1),
    )(x, g, wq, k_all, v_all, wo)


FF_SPLIT = 2


def _ffn_kernel(x_ref, g_ref, wg_ref, wu_ref, wd_ref, o_ref):
    x = x_ref[...]
    h = _rms(x, g_ref[...]).astype(bf16)
    fc = D_FF // FF_SPLIT
    acc = x
    for j in range(FF_SPLIT):
        cs = slice(j * fc, (j + 1) * fc)
        gate = jnp.dot(h, wg_ref[:, cs], preferred_element_type=f32)
        up = jnp.dot(h, wu_ref[:, cs], preferred_element_type=f32)
        acc = acc + _dot(jax.nn.silu(gate) * up, wd_ref[cs, :])
    o_ref[...] = acc


def _ffn(name, x, g, wg, wu, wd, layer, tm):
    m = x.shape[0]
    row = pl.BlockSpec((tm, D), lambda i: (i, 0))
    return pl.pallas_call(
        _ffn_kernel,
        name=name,
        out_shape=jax.ShapeDtypeStruct((m, D), f32),
        grid=(m // tm,),
        in_specs=[row, _layer_spec((1, D), layer), _layer_spec((D, D_FF), layer), _layer_spec((D, D_FF), layer),
                  _layer_spec((D_FF, D), layer)],
        out_specs=row,
        compiler_params=_cp(1),
    )(x, g, wg, wu, wd)


def _prep_w_in_kernel(w_ref, mix_ref, gate_ref, *, o1, o2):
    zpad = jnp.zeros((w_ref.shape[0], LANE - GLA_RANK), bf16)
    lo = O_GLA + W_GLA - LANE
    mix_ref[:, 0:lo] = w_ref[:, 0:lo]
    mix_ref[:, lo:O_SSD] = jnp.concatenate([w_ref[:, lo:o1], zpad], axis=1)
    mix_ref[:, O_SSD:W_MIX] = jnp.concatenate([w_ref[:, o1:o2], zpad], axis=1)
    gate_ref[...] = w_ref[:, o2:]


def _prep_w_in(w_in, o1, o2):
    depth, k, n = w_in.shape
    assert o1 - (O_GLA + W_GLA - LANE) == GLA_RANK and o2 - o1 == W_SSD - (LANE - SSD_HEADS) and n - o2 == W_GATE
    tr = 256
    idx = lambda l, i: (l, i, 0)
    return pl.pallas_call(
        functools.partial(_prep_w_in_kernel, o1=o1, o2=o2),
        name="prep_w_in",
        grid=(depth, k // tr),
        in_specs=[pl.BlockSpec((None, tr, n), idx)],
        out_specs=(pl.BlockSpec((None, tr, W_MIX), idx), pl.BlockSpec((None, tr, W_GATE), idx)),
        out_shape=(jax.ShapeDtypeStruct((depth, k, W_MIX), bf16), jax.ShapeDtypeStruct((depth, k, W_GATE), bf16)),
        compiler_params=_cp(2),
    )(w_in)


def _block_diag_tiles(w):
    depth, nblk, bs, _ = w.shape
    per = LRU_TILE // bs
    w4 = w.reshape(depth, nblk // per, per, bs, bs)
    eye = jnp.eye(per, dtype=w.dtype)
    t = jnp.einsum("ltpij,pq->ltpiqj", w4, eye)
    return t.reshape(depth, nblk // per, LRU_TILE, LRU_TILE).astype(bf16)


def _rows(a):
    return a[:, None, :]


def _pad_last(a, width):
    return jnp.pad(a, [(0, 0)] * (a.ndim - 1) + [(0, width - a.shape[-1])])


def kernel(x_prompt, x_sample, mem_prompt, state_lru_h, state_lru_conv, state_gla_S, state_ssd_h, state_ssd_conv, cache_mem_k, cache_mem_v, norm_mix, w_in, lru_conv_w, lru_conv_b, lru_wa, lru_ba, lru_wx, lru_bx, lru_lambda, w_lru_out, gla_w_alpha, gla_b_alpha, gla_norm, w_gla_out, ssd_conv_w, ssd_conv_b, ssd_dt_bias, ssd_a_log, ssd_d, ssd_norm, w_ssd_out, w_mix_out, norm_xattn, norm_mem, w_xq, w_xk, w_xv, w_xo, norm_ffn, w_ffn_gate, w_ffn_up, w_ffn_down, norm_final):
    bp, lp, _ = x_prompt.shape
    bs, ls, _ = x_sample.shape
    assert ls == SUB, "sample kernels treat each 8-row group as one sequence"
    hk = GLA_HEADS * GLA_DK
    xp = x_prompt.reshape(bp * lp, D)
    xs = x_sample.reshape(bs * ls, D)
    mem = mem_prompt.reshape(bp * MEM, D)

    o1 = 2 * D + 2 * hk + 2 * D + GLA_RANK
    o2 = o1 + D + SSD_CH + SSD_HEADS
    w_mix_in, w_gate = _prep_w_in(w_in.astype(bf16), o1, o2)
    g_mix, g_xa, g_mem, g_ffn = _rows(norm_mix), _rows(norm_xattn), _rows(norm_mem), _rows(norm_ffn)
    lru_params = (lru_conv_w, _rows(lru_conv_b), _rows(lru_lambda), _block_diag_tiles(lru_wa), _rows(lru_ba),
                  _block_diag_tiles(lru_wx), _rows(lru_bx))
    gla_params = (jnp.pad(gla_w_alpha, ((0, 0), (0, LANE - GLA_RANK), (0, 0))).astype(bf16),
                  _rows(gla_b_alpha), _rows(gla_norm))
    expand = (jnp.arange(LANE)[:, None] == (jnp.arange(D)[None, :] // SSD_P)).astype(bf16)
    ssd_params = (ssd_conv_w, _rows(ssd_conv_b), _rows(_pad_last(ssd_dt_bias, LANE)),
                  _rows(_pad_last(ssd_a_log, LANE)), _rows(jnp.repeat(ssd_d, SSD_P, axis=1)),
                  _rows(ssd_norm), expand)
    w_lo, w_go, w_so, w_mo = (w.astype(bf16) for w in (w_lru_out, w_gla_out, w_ssd_out, w_mix_out))
    wq, wo = w_xq.astype(bf16), w_xo.astype(bf16)
    wkv = jnp.concatenate([w_xk, w_xv], axis=2).astype(bf16)
    wfg, wfu, wfd = w_ffn_gate.astype(bf16), w_ffn_up.astype(bf16), w_ffn_down.astype(bf16)
    pad5 = ((0, 0), (0, 0), (SUB - (CONV_W - 1), 0), (0, 0))
    bp_lru = jnp.pad(state_lru_conv, pad5).reshape(DEPTH, bs * ls, D)
    bp_ssd = jnp.pad(state_ssd_conv, pad5).reshape(DEPTH, bs * ls, SSD_CH)
    h0_lru = state_lru_h[:, :, None, :]

    p_out = [[] for _ in range(7)]
    s_out = [[] for _ in range(5)]
    for l in range(DEPTH):
        kv, k4, v4 = _mem_kv(mem, g_mem, wkv, l, T_ROW)
        yl, yg, ys, p_h, p_tail, p_s, p_sh, p_stail = _mix_prompt(
            xp, g_mix, w_mix_in, lru_params, gla_params, ssd_params, l, bp, lp, T_MIX)
        xp = _merge("merge_prompt", xp, g_mix, w_gate, yl, yg, ys, w_lo, w_go, w_so, w_mo, l, T_MERGE)
        xp = _xattn_prompt(xp, g_xa, wq, kv.reshape(bp, MEM, 2 * D), wo, l, bp, lp, T_XATTN)
        xp = _ffn("ffn_prompt", xp, g_ffn, wfg, wfu, wfd, l, T_FFN)
        p_vals = (p_h[:, 0], p_tail[:, SUB - 3:], p_s, p_sh, p_stail[:, SUB - 3:],
                  k4.reshape(bp, MEM, XA_HEADS, XA_HD), v4.reshape(bp, MEM, XA_HEADS, XA_HD))
        for lst, val in zip(p_out, p_vals):
            lst.append(val)

        u_lru, u_gla, u_ssd = _in_proj_sample(xs, g_mix, w_mix_in, l, T_SAMPLE)
        yl, h_last = _lru_sample(u_lru, h0_lru, bp_lru[l], lru_params, l, T_SAMPLE)
        yg, s_s = _gla_sample(u_gla, state_gla_S, gla_params, l, NB_STATE, ls)
        ys, s_sh = _ssd_sample(u_ssd, bp_ssd[l], state_ssd_h, ssd_params, l, NB_STATE, ls)
        xs = _merge("merge_sample", xs, g_mix, w_gate, yl, yg, ys, w_lo, w_go, w_so, w_mo, l, T_SAMPLE)
        xs = _xattn_sample(xs, g_xa, wq, cache_mem_k, cache_mem_v, wo, l, NB_STATE, ls)
        xs = _ffn("ffn_sample", xs, g_ffn, wfg, wfu, wfd, l, T_SAMPLE)
        s_vals = (h_last[:, 0], u_lru.reshape(bs, ls, W_LRU)[:, ls - 3:, :D], s_s, s_sh,
                  u_ssd.reshape(bs, ls, W_SSD)[:, ls - 3:, D:D + SSD_CH])
        for lst, val in zip(s_out, s_vals):
            lst.append(val)

    y_prompt = _final_norm("final_norm_prompt", xp, norm_final.reshape(1, D), T_ROW).reshape(bp, lp, D)
    y_sample = _final_norm("final_norm_sample", xs, norm_final.reshape(1, D), T_SAMPLE).reshape(bs, ls, D)
    p_stack = [jnp.stack(t, axis=0) for t in p_out]
    s_stack = [jnp.stack(t, axis=0) for t in s_out]
    return (y_prompt, y_sample, *p_stack, *s_stack)
```

```python
import functools

import jax
import jax.numpy as jnp
from jax import lax
from jax.experimental import pallas as pl
from jax.experimental.pallas import tpu as pltpu

f32 = jnp.float32
bf16 = jnp.bfloat16

D = 1024
DEPTH = 4
EPS = 1e-6
CONV_W = 4
LRU_C = 8.0
LRU_TILE = 256
GLA_HEADS, GLA_DK, GLA_DV, GLA_RANK, GLA_TAU, GLA_CHUNK = 4, 128, 256, 16, 16.0, 64
SSD_HEADS, SSD_P, SSD_GROUPS, SSD_N, SSD_CHUNK = 16, 64, 2, 64, 128
SSD_HP = SSD_HEADS // SSD_GROUPS * SSD_P
SSD_CH = D + 2 * SSD_GROUPS * SSD_N
XA_HEADS, XA_HD, MEM = 4, 256, 256
D_FF = 2816
LANE = 128
SUB = 8
W_LRU, W_GLA, W_SSD, W_GATE = 2 * D, 3 * D + LANE, D + SSD_CH + LANE, 3 * D
W_MIX = W_LRU + W_GLA + W_SSD
O_GLA, O_SSD = W_LRU, W_LRU + W_GLA
VMEM_LIMIT = 56 * 1024 * 1024

T_MIX = 256
T_ROW = 512
T_XATTN = 1024
T_FFN = 1024
T_MERGE = 512
T_SAMPLE = 256
NB_STATE = 8


def _cp(n_axes):
    return pltpu.CompilerParams(dimension_semantics=("arbitrary",) * n_axes,
                                vmem_limit_bytes=VMEM_LIMIT)


def _const_spec(shape):
    nd = len(shape)
    return pl.BlockSpec(shape, lambda *_: (0,) * nd, pipeline_mode=pl.Buffered(1))


def _layer_spec(shape, layer):
    nd = len(shape)
    return pl.BlockSpec((None,) + tuple(shape), lambda *_: (layer,) + (0,) * nd,
                        pipeline_mode=pl.Buffered(1))


def _rms(x, g):
    return x * lax.rsqrt(jnp.mean(x * x, axis=-1, keepdims=True) + EPS) * g


def _dot(a, b):
    return jnp.dot(a.astype(bf16), b.astype(bf16), preferred_element_type=f32)


def _dot_nt(a, b):
    return lax.dot_general(a.astype(bf16), b.astype(bf16), (((1,), (1,)), ((), ())),
                           preferred_element_type=f32)


def _split(x):
    hi = x.astype(bf16)
    return hi, (x - hi.astype(f32)).astype(bf16)


def _dot_split(m, x):
    hi, lo = _split(x)
    return (jnp.dot(m, hi, preferred_element_type=f32) + jnp.dot(m, lo, preferred_element_type=f32))


def _dot_split_rows(x, m):
    hi, lo = _split(x)
    return jnp.dot(hi, m, preferred_element_type=f32) + jnp.dot(lo, m, preferred_element_type=f32)


def _bdot(a, b, nb, rows, mode):
    a = a.astype(bf16)
    b = b.astype(bf16)
    if mode == "tn":
        if nb == 1:
            return lax.dot_general(a, b, (((0,), (0,)), ((), ())), preferred_element_type=f32)[None]
        a3 = a.reshape(nb, rows, a.shape[-1])
        b3 = b.reshape(nb, rows, b.shape[-1])
        return jnp.einsum("bki,bkj->bij", a3, b3, preferred_element_type=f32)
    if nb == 1:
        if mode == "nn":
            return jnp.dot(a, b[0], preferred_element_type=f32)
        return lax.dot_general(a, b[0], (((1,), (1,)), ((), ())), preferred_element_type=f32)
    a3 = a.reshape(nb, rows, a.shape[-1])
    eq = "bik,bkj->bij" if mode == "nn" else "bik,bjk->bij"
    out = jnp.einsum(eq, a3, b, preferred_element_type=f32)
    return out.reshape(nb * rows, out.shape[-1])


def _seq_masks(r, c):
    ri = lax.broadcasted_iota(jnp.int32, (r, r), 0)
    ci = lax.broadcasted_iota(jnp.int32, (r, r), 1)
    if r == c:
        same = jnp.full((r, r), True)
    else:
        shift = c.bit_length() - 1
        same = (ri >> shift) == (ci >> shift)
    causal = same & (ri >= ci)
    return causal, same


def _conv_carry(x, tail, w_ref, b_ref):
    xe = jnp.concatenate([tail, x], axis=0)
    y = b_ref[...] + w_ref[0:1, :] * pltpu.roll(xe, 3, 0)[SUB:]
    y = y + w_ref[1:2, :] * pltpu.roll(xe, 2, 0)[SUB:]
    y = y + w_ref[2:3, :] * pltpu.roll(xe, 1, 0)[SUB:]
    return y + w_ref[3:4, :] * x


def _conv_groups(x, bufpad, w_ref, b_ref):
    rows = x.shape[0]
    l = lax.broadcasted_iota(jnp.int32, (rows, 1), 0) & (SUB - 1)
    y = b_ref[...]
    for k in range(CONV_W - 1):
        s = CONV_W - 1 - k
        sh = jnp.where(l >= s, pltpu.roll(x, s, 0), pltpu.roll(bufpad, rows - SUB + s, 0))
        y = y + w_ref[k:k + 1, :] * sh
    return y + w_ref[3:4, :] * x


def _in_proj_sample_kernel(x_ref, g_ref, w_ref, ul_ref, ug_ref, us_ref):
    h = _rms(x_ref[...], g_ref[...]).astype(bf16)
    ul_ref[...] = jnp.dot(h, w_ref[:, 0:W_LRU], preferred_element_type=f32)
    ug_ref[...] = jnp.dot(h, w_ref[:, O_GLA:O_GLA + W_GLA], preferred_element_type=f32)
    us_ref[...] = jnp.dot(h, w_ref[:, O_SSD:W_MIX], preferred_element_type=f32)


def _in_proj_sample(x, g, w, layer, tm):
    m, k = x.shape
    row = lambda i: (i, 0)
    return pl.pallas_call(
        _in_proj_sample_kernel,
        name="in_proj_sample",
        out_shape=tuple(jax.ShapeDtypeStruct((m, n), f32) for n in (W_LRU, W_GLA, W_SSD)),
        grid=(m // tm,),
        in_specs=[pl.BlockSpec((tm, k), row), _layer_spec((1, k), layer), _layer_spec((k, W_MIX), layer)],
        out_specs=tuple(pl.BlockSpec((tm, n), row) for n in (W_LRU, W_GLA, W_SSD)),
        compiler_params=_cp(1),
    )(x, g, w)


def _mem_kv_kernel(x_ref, g_ref, w_ref, kv_ref, k4_ref, v4_ref):
    kv = _dot(_rms(x_ref[...], g_ref[...]), w_ref[...])
    kv_ref[...] = kv
    rows = kv.shape[0]
    k4_ref[...] = kv[:, :D].reshape(rows, XA_HEADS, XA_HD)
    v4_ref[...] = kv[:, D:].reshape(rows, XA_HEADS, XA_HD)


def _mem_kv(mem, g, wkv, layer, tm):
    m, k = mem.shape
    row = lambda i: (i, 0)
    row3 = lambda i: (i, 0, 0)
    return pl.pallas_call(
        _mem_kv_kernel,
        name="mem_kv",
        out_shape=(jax.ShapeDtypeStruct((m, 2 * D), f32), jax.ShapeDtypeStruct((m, XA_HEADS, XA_HD), f32),
                   jax.ShapeDtypeStruct((m, XA_HEADS, XA_HD), f32)),
        grid=(m // tm,),
        in_specs=[pl.BlockSpec((tm, k), row), _layer_spec((1, k), layer), _layer_spec((k, 2 * D), layer)],
        out_specs=(pl.BlockSpec((tm, 2 * D), row), pl.BlockSpec((tm, XA_HEADS, XA_HD), row3),
                   pl.BlockSpec((tm, XA_HEADS, XA_HD), row3)),
        compiler_params=_cp(1),
    )(mem, g, wkv)


def _final_norm_kernel(x_ref, g_ref, o_ref):
    o_ref[...] = _rms(x_ref[...], g_ref[...])


def _final_norm(name, x, g, tm):
    m, k = x.shape
    return pl.pallas_call(
        _final_norm_kernel,
        name=name,
        out_shape=jax.ShapeDtypeStruct((m, k), f32),
        grid=(m // tm,),
        in_specs=[pl.BlockSpec((tm, k), lambda i: (i, 0)), _const_spec((1, k))],
        out_specs=pl.BlockSpec((tm, k), lambda i: (i, 0)),
        compiler_params=_cp(1),
    )(x, g)


def _lru_decay_and_scale(log_a):
    a = jnp.exp(log_a)
    m = 1.0 - a * a
    return a, jnp.where(m > 0.0, m * lax.rsqrt(m), 0.0)


def _lru_math(xc, lam_ref, wa_ref, ba_ref, wx_ref, bx_ref):
    xcb = xc.astype(bf16)
    nt = D // LRU_TILE
    r_pre = jnp.concatenate([jnp.dot(xcb[:, i * LRU_TILE:(i + 1) * LRU_TILE], wa_ref[i],
                                     preferred_element_type=f32) for i in range(nt)], axis=1)
    i_pre = jnp.concatenate([jnp.dot(xcb[:, i * LRU_TILE:(i + 1) * LRU_TILE], wx_ref[i],
                                     preferred_element_type=f32) for i in range(nt)], axis=1)
    r = jax.nn.sigmoid(r_pre + ba_ref[...])
    ig = jax.nn.sigmoid(i_pre + bx_ref[...])
    a, mult = _lru_decay_and_scale(-LRU_C * r * jax.nn.softplus(-lam_ref[...]))
    return a, mult * (ig * xc)


def _group_scan(a, b):
    l = lax.broadcasted_iota(jnp.int32, (a.shape[0], 1), 0) & (SUB - 1)
    for s in (1, 2, 4):
        m = l >= s
        b = jnp.where(m, a * pltpu.roll(b, s, 0) + b, b)
        a = jnp.where(m, a * pltpu.roll(a, s, 0), a)
    return a, b


def _lru_prompt_tiles(hn, w_ref, p, yl_ref, lh_ref, ltail_ref, hc_scr, ltail_scr):
    cw_ref, cb_ref, lam_ref, wa_ref, ba_ref, wx_ref, bx_ref = p
    rows = hn.shape[0]
    for ci in range(D // LRU_TILE):
        cs = slice(ci * LRU_TILE, (ci + 1) * LRU_TILE)
        x = jnp.dot(hn, w_ref[:, cs], preferred_element_type=f32)
        gate = jnp.dot(hn, w_ref[:, D + ci * LRU_TILE:D + (ci + 1) * LRU_TILE], preferred_element_type=f32)
        xe = jnp.concatenate([ltail_scr[:, cs], x], axis=0)
        xc = cb_ref[:, cs] + cw_ref[0:1, cs] * pltpu.roll(xe, 3, 0)[SUB:]
        xc = xc + cw_ref[1:2, cs] * pltpu.roll(xe, 2, 0)[SUB:]
        xc = xc + cw_ref[2:3, cs] * pltpu.roll(xe, 1, 0)[SUB:]
        xc = xc + cw_ref[3:4, cs] * x
        xcb = xc.astype(bf16)
        r = jax.nn.sigmoid(jnp.dot(xcb, wa_ref[ci], preferred_element_type=f32) + ba_ref[:, cs])
        ig = jax.nn.sigmoid(jnp.dot(xcb, wx_ref[ci], preferred_element_type=f32) + bx_ref[:, cs])
        a, mult = _lru_decay_and_scale(-LRU_C * r * jax.nn.softplus(-lam_ref[:, cs]))
        b = mult * (ig * xc)
        a, b = _group_scan(a, b)
        carry = hc_scr[0:1, cs]
        hs = []
        for j in range(rows // SUB):
            hj = a[j * SUB:(j + 1) * SUB] * carry + b[j * SUB:(j + 1) * SUB]
            carry = hj[SUB - 1:SUB]
            hs.append(hj)
        yl_ref[:, cs] = jnp.concatenate(hs, axis=0) * jax.nn.gelu(gate)
        hc = jnp.broadcast_to(carry, (SUB, LRU_TILE))
        hc_scr[:, cs] = hc
        lh_ref[0, :, cs] = hc
        ltail = x[rows - SUB:]
        ltail_scr[:, cs] = ltail
        ltail_ref[0, :, cs] = ltail


def _lru_sample_kernel(u_ref, h0_ref, bp_ref, cw_ref, cb_ref, lam_ref, wa_ref, ba_ref, wx_ref, bx_ref,
                       y_ref, h_ref):
    x = u_ref[:, 0:D]
    gate = u_ref[:, D:W_LRU]
    xc = _conv_groups(x, bp_ref[...], cw_ref, cb_ref)
    a, b = _lru_math(xc, lam_ref, wa_ref, ba_ref, wx_ref, bx_ref)
    a, b = _group_scan(a, b)
    nseq = x.shape[0] // SUB
    h3 = a.reshape(nseq, SUB, D) * h0_ref[...] + b.reshape(nseq, SUB, D)
    h_ref[...] = h3[:, SUB - 1:SUB, :]
    y_ref[...] = h3.reshape(nseq * SUB, D) * jax.nn.gelu(gate)


def _lru_param_specs(layer):
    nt = D // LRU_TILE
    return [_layer_spec((CONV_W, D), layer), _layer_spec((1, D), layer), _layer_spec((1, D), layer),
            _layer_spec((nt, LRU_TILE, LRU_TILE), layer), _layer_spec((1, D), layer),
            _layer_spec((nt, LRU_TILE, LRU_TILE), layer), _layer_spec((1, D), layer)]


def _lru_sample(u, h0, bufpad, params, layer, tm):
    m = u.shape[0]
    row = lambda i: (i, 0)
    seq3 = lambda i: (i, 0, 0)
    return pl.pallas_call(
        _lru_sample_kernel,
        name="lru_sample",
        out_shape=(jax.ShapeDtypeStruct((m, D), f32), jax.ShapeDtypeStruct((m // SUB, 1, D), f32)),
        grid=(m // tm,),
        in_specs=[pl.BlockSpec((tm, W_LRU), row),
                  pl.BlockSpec((None, tm // SUB, 1, D), lambda i: (layer, i, 0, 0)),
                  pl.BlockSpec((tm, D), row)] + _lru_param_specs(layer),
        out_specs=(pl.BlockSpec((tm, D), row), pl.BlockSpec((tm // SUB, 1, D), seq3)),
        compiler_params=_cp(1),
    )(u, h0, bufpad, *params)


def _gla_core(u, s_list, nb, c, p, chain=False):
    wal_ref, bal_ref, gn_ref = p
    r_rows = nb * c
    hk = GLA_HEADS * GLA_DK
    q = u[:, 0:hk]
    k = u[:, hk:2 * hk]
    v = u[:, 2 * hk:2 * hk + D]
    rg = u[:, 2 * hk + D:2 * hk + 2 * D]
    glow = u[:, 2 * hk + 2 * D:]
    z = _dot(glow, wal_ref[...]) + bal_ref[...]
    g = jax.nn.log_sigmoid(z) / GLA_TAU
    causal, same = _seq_masks(r_rows, c)
    g_hi, g_lo = _split(g)
    tril = causal.astype(bf16)
    ones_sq = same.astype(bf16)
    bcum = jnp.dot(tril, g_hi, preferred_element_type=f32) + jnp.dot(tril, g_lo, preferred_element_type=f32)
    btot = jnp.dot(ones_sq, g_hi, preferred_element_type=f32) + jnp.dot(ones_sq, g_lo, preferred_element_type=f32)
    qt = (q * GLA_DK ** -0.5) * jnp.exp(bcum)
    kt = k * jnp.exp(-bcum)
    ke = k * jnp.exp(btot - bcum)
    ones_v = jnp.ones((r_rows, GLA_DV), bf16)
    if chain:
        assert nb <= SUB
        tot8 = jnp.concatenate([btot[j * c:j * c + 1, :] for j in range(nb)] + [btot[0:SUB - nb, :]], axis=0)
        dec_t = jnp.exp(tot8.T)
    outs, new_s = [], []
    for h in range(GLA_HEADS):
        ks = slice(h * GLA_DK, (h + 1) * GLA_DK)
        vs = slice(h * GLA_DV, (h + 1) * GLA_DV)
        s_h = s_list[h]
        att = jnp.where(causal, _dot_nt(qt[:, ks], kt[:, ks]), 0.0)
        upd = _bdot(ke[:, ks], v[:, vs], nb, c, "tn")
        if chain:
            dec = [dec_t[h * GLA_DK:(h + 1) * GLA_DK, j:j + 1] for j in range(nb)]
        else:
            blast = _bdot(g_hi[:, ks], ones_v, nb, c, "tn") + _bdot(g_lo[:, ks], ones_v, nb, c, "tn")
            dec = jnp.exp(blast)
        if chain:
            cur = s_h[0]
            starts = []
            for j in range(nb):
                starts.append(cur)
                cur = dec[j] * cur + upd[j]
            s_in = jnp.stack(starts, axis=0)
            new_s.append(cur[None])
        else:
            s_in = s_h
            new_s.append(dec * s_h + upd)
        o_h = _dot(att, v[:, vs]) + _bdot(qt[:, ks], s_in, nb, c, "nn")
        o_h = o_h * lax.rsqrt(jnp.mean(o_h * o_h, axis=-1, keepdims=True) + EPS) * gn_ref[...]
        outs.append(o_h)
    o = jnp.concatenate(outs, axis=1)
    return o * jax.nn.silu(rg), new_s


def _gla_sample_kernel(u_ref, s0_ref, wal_ref, bal_ref, gn_ref, *refs, nb, c):
    o_ref, s_ref = refs[-2:]
    s_list = [s0_ref[:, h] for h in range(GLA_HEADS)]
    o, new_s = _gla_core(u_ref[...], s_list, nb, c, (wal_ref, bal_ref, gn_ref))
    o_ref[...] = o
    for h in range(GLA_HEADS):
        s_ref[:, h] = new_s[h]


def _gla_param_specs(layer):
    hk = GLA_HEADS * GLA_DK
    return [_layer_spec((LANE, hk), layer), _layer_spec((1, hk), layer), _layer_spec((1, GLA_DV), layer)]


def _gla_sample(u, s0_all, params, layer, nb, c, s_new_all=None):
    nseq = s0_all.shape[1]
    blk = (None, nb, GLA_HEADS, GLA_DK, GLA_DV)
    sidx = lambda i: (layer, i, 0, 0, 0)
    in_specs = [pl.BlockSpec((nb * c, W_GLA), lambda i: (i, 0)), pl.BlockSpec(blk, sidx)] + _gla_param_specs(layer)
    args = [u, s0_all, *params]
    aliases = {}
    if s_new_all is not None:
        aliases = {len(args): 1}
        in_specs.append(pl.BlockSpec(memory_space=pl.ANY))
        args.append(s_new_all)
    return pl.pallas_call(
        functools.partial(_gla_sample_kernel, nb=nb, c=c),
        name="gla_sample",
        out_shape=(jax.ShapeDtypeStruct((nseq * c, D), f32), jax.ShapeDtypeStruct(s0_all.shape, f32)),
        grid=(nseq // nb,),
        in_specs=in_specs,
        out_specs=(pl.BlockSpec((nb * c, D), lambda i: (i, 0)), pl.BlockSpec(blk, sidx)),
        input_output_aliases=aliases,
        compiler_params=_cp(1),
    )(*args)


def _ssd_core(sz, xbc, sdt, h_list, nb, c, p):
    dtb_ref, alog_ref, dskip_ref, nrm_ref, exp_ref = p
    r_rows = nb * c
    sx = xbc[:, :D]
    dt = jax.nn.softplus(sdt + dtb_ref[...])
    da = dt * (-jnp.exp(alog_ref[...]))
    causal, same = _seq_masks(r_rows, c)
    cum = _dot_split(causal.astype(bf16), da)
    tot = _dot_split(same.astype(bf16), da)
    ecum = jnp.exp(cum)
    wgt = jnp.exp(tot - cum) * dt
    expd = _dot_split_rows(jnp.concatenate([ecum, wgt, da], axis=0), exp_ref[...])
    ecum_x = expd[0:r_rows]
    wgt_x = expd[r_rows:2 * r_rows]
    da_x = expd[2 * r_rows:]
    cum_t = cum.T
    dt_t = dt.T
    lane = lax.broadcasted_iota(jnp.int32, (r_rows, LANE), 1)
    xw = sx * wgt_x
    da_hi, da_lo = _split(da_x)
    ones_n = jnp.ones((r_rows, SSD_N), bf16)
    ys, new_h = [], []
    for grp in range(SSD_GROUPS):
        bm = xbc[:, D + grp * SSD_N:D + (grp + 1) * SSD_N]
        cm = xbc[:, D + (SSD_GROUPS + grp) * SSD_N:D + (SSD_GROUPS + grp + 1) * SSD_N]
        cb = _dot_nt(cm, bm)
        gs = slice(grp * SSD_HP, (grp + 1) * SSD_HP)
        y_inter = _bdot(cm, h_list[grp], nb, c, "nt") * ecum_x[:, gs]
        pairs = []
        for pr in range(SSD_HP // LANE):
            col = grp * SSD_HP + pr * LANE
            xp = sx[:, col:col + LANE]
            acc = None
            for half in range(2):
                hd = (col // SSD_P) + half
                seg = cum[:, hd:hd + 1] - cum_t[hd:hd + 1, :]
                m = cb * jnp.exp(jnp.where(causal, seg, -jnp.inf)) * dt_t[hd:hd + 1, :]
                xh = jnp.where((lane < SSD_P) if half == 0 else (lane >= SSD_P), xp, 0.0)
                part = _dot(m, xh)
                acc = part if acc is None else acc + part
            pairs.append(acc)
        ys.append(jnp.concatenate(pairs, axis=1) + y_inter)
        dec = jnp.exp(_bdot(da_hi[:, gs], ones_n, nb, c, "tn") + _bdot(da_lo[:, gs], ones_n, nb, c, "tn"))
        new_h.append(dec * h_list[grp] + _bdot(xw[:, gs], bm, nb, c, "tn"))
    y = jnp.concatenate(ys, axis=1) + dskip_ref[...] * sx
    y = y * jax.nn.silu(sz)
    halves = []
    for grp in range(SSD_GROUPS):
        yh = y[:, grp * SSD_HP:(grp + 1) * SSD_HP]
        halves.append(yh * lax.rsqrt(jnp.mean(yh * yh, axis=-1, keepdims=True) + EPS))
    return jnp.concatenate(halves, axis=1) * nrm_ref[...], new_h


def _ssd_sample_kernel(u_ref, bp_ref, h0_ref, cw_ref, cb_ref, dtb_ref, alog_ref, dskip_ref, nrm_ref, exp_ref,
                       y_ref, h_ref, *, nb, c):
    sz = u_ref[:, 0:D]
    raw = u_ref[:, D:D + SSD_CH]
    sdt = u_ref[:, D + SSD_CH:W_SSD]
    xbc = jax.nn.silu(_conv_groups(raw, bp_ref[...], cw_ref, cb_ref))
    hpg = SSD_HEADS // SSD_GROUPS
    h_list = [h0_ref[:, grp * hpg:(grp + 1) * hpg].reshape(nb, SSD_HP, SSD_N) for grp in range(SSD_GROUPS)]
    y, new_h = _ssd_core(sz, xbc, sdt, h_list, nb, c, (dtb_ref, alog_ref, dskip_ref, nrm_ref, exp_ref))
    y_ref[...] = y
    for grp in range(SSD_GROUPS):
        h_ref[:, grp * hpg:(grp + 1) * hpg] = new_h[grp].reshape(nb, hpg, SSD_P, SSD_N)


def _ssd_param_specs(layer):
    return [_layer_spec((CONV_W, SSD_CH), layer), _layer_spec((1, SSD_CH), layer), _layer_spec((1, LANE), layer),
            _layer_spec((1, LANE), layer), _layer_spec((1, D), layer), _layer_spec((1, D), layer),
            _const_spec((LANE, D))]


def _ssd_sample(u, bufpad, h0_all, params, layer, nb, c):
    nseq = h0_all.shape[1]
    blk = (nb, SSD_HEADS, SSD_P, SSD_N)
    row = lambda i: (i, 0)
    return pl.pallas_call(
        functools.partial(_ssd_sample_kernel, nb=nb, c=c),
        name="ssd_sample",
        out_shape=(jax.ShapeDtypeStruct((nseq * c, D), f32), jax.ShapeDtypeStruct(h0_all.shape[1:], f32)),
        grid=(nseq // nb,),
        in_specs=[pl.BlockSpec((nb * c, W_SSD), row), pl.BlockSpec((nb * c, SSD_CH), row),
                  pl.BlockSpec((None,) + blk, lambda i: (layer, i, 0, 0, 0))] + _ssd_param_specs(layer),
        out_specs=(pl.BlockSpec((nb * c, D), row), pl.BlockSpec(blk, lambda i: (i, 0, 0, 0))),
        compiler_params=_cp(1),
    )(u, bufpad, h0_all, *params)


def _mix_prompt_kernel(x_ref, g_ref, w_ref, *refs):
    lru_p = refs[0:7]
    gla_p = refs[7:10]
    ssd_cw, ssd_cb = refs[10:12]
    ssd_p = refs[12:17]
    yl_ref, yg_ref, ys_ref, lh_ref, ltail_ref, gs_ref, sh_ref, stail_ref = refs[17:25]
    hc_scr, ltail_scr, s_scr, h_scr, stail_scr = refs[25:30]
    t = pl.program_id(1)

    @pl.when(t == 0)
    def _():
        hc_scr[...] = jnp.zeros_like(hc_scr)
        ltail_scr[...] = jnp.zeros_like(ltail_scr)
        s_scr[...] = jnp.zeros_like(s_scr)
        h_scr[...] = jnp.zeros_like(h_scr)
        stail_scr[...] = jnp.zeros_like(stail_scr)

    hn = _rms(x_ref[...], g_ref[...]).astype(bf16)
    rows = hn.shape[0]

    _lru_prompt_tiles(hn, w_ref, lru_p, yl_ref, lh_ref, ltail_ref, hc_scr, ltail_scr)

    u = jnp.dot(hn, w_ref[:, O_GLA:O_GLA + W_GLA], preferred_element_type=f32)
    s_list = [s_scr[h][None] for h in range(GLA_HEADS)]
    o, s_list = _gla_core(u, s_list, rows // GLA_CHUNK, GLA_CHUNK, gla_p, chain=True)
    yg_ref[...] = o
    for h in range(GLA_HEADS):
        s_scr[h] = s_list[h][0]

    u = jnp.dot(hn, w_ref[:, O_SSD:O_SSD + W_SSD], preferred_element_type=f32)
    sz = u[:, :D]
    raw = u[:, D:D + SSD_CH]
    sdt = u[:, D + SSD_CH:]
    xbc = jax.nn.silu(_conv_carry(raw, stail_scr[...], ssd_cw, ssd_cb))
    h_list = [h_scr[grp][None] for grp in range(SSD_GROUPS)]
    c = SSD_CHUNK
    for j in range(rows // c):
        rs = slice(j * c, (j + 1) * c)
        y, h_list = _ssd_core(sz[rs], xbc[rs], sdt[rs], h_list, 1, c, ssd_p)
        ys_ref[rs, :] = y
    for grp in range(SSD_GROUPS):
        h_scr[grp] = h_list[grp][0]
    stail = raw[rows - SUB:]
    stail_scr[...] = stail
    stail_ref[0] = stail

    @pl.when(t == pl.num_programs(1) - 1)
    def _():
        gs_ref[0] = s_scr[...]
        hpg = SSD_HEADS // SSD_GROUPS
        for grp in range(SSD_GROUPS):
            sh_ref[0, grp * hpg:(grp + 1) * hpg] = h_scr[grp].reshape(hpg, SSD_P, SSD_N)


def _mix_prompt(x, g, w, lru_params, gla_params, ssd_params, layer, nseq, seqlen, tt):
    nt = seqlen // tt
    row = lambda b, t: (b * nt + t, 0)
    seq3 = lambda b, t: (b, 0, 0)
    seq4 = lambda b, t: (b, 0, 0, 0)
    rows = nseq * seqlen
    return pl.pallas_call(
        _mix_prompt_kernel,
        name="mix_prompt",
        out_shape=(jax.ShapeDtypeStruct((rows, D), f32), jax.ShapeDtypeStruct((rows, D), f32),
                   jax.ShapeDtypeStruct((rows, D), f32),
                   jax.ShapeDtypeStruct((nseq, SUB, D), f32), jax.ShapeDtypeStruct((nseq, SUB, D), f32),
                   jax.ShapeDtypeStruct((nseq, GLA_HEADS, GLA_DK, GLA_DV), f32),
                   jax.ShapeDtypeStruct((nseq, SSD_HEADS, SSD_P, SSD_N), f32),
                   jax.ShapeDtypeStruct((nseq, SUB, SSD_CH), f32)),
        grid=(nseq, nt),
        in_specs=[pl.BlockSpec((tt, D), row), _layer_spec((1, D), layer), _layer_spec((D, W_MIX), layer)]
        + _lru_param_specs(layer) + _gla_param_specs(layer) + _ssd_param_specs(layer),
        out_specs=(pl.BlockSpec((tt, D), row), pl.BlockSpec((tt, D), row), pl.BlockSpec((tt, D), row),
                   pl.BlockSpec((1, SUB, D), seq3), pl.BlockSpec((1, SUB, D), seq3),
                   pl.BlockSpec((1, GLA_HEADS, GLA_DK, GLA_DV), seq4),
                   pl.BlockSpec((1, SSD_HEADS, SSD_P, SSD_N), seq4),
                   pl.BlockSpec((1, SUB, SSD_CH), seq3)),
        scratch_shapes=[pltpu.VMEM((SUB, D), f32), pltpu.VMEM((SUB, D), f32),
                        pltpu.VMEM((GLA_HEADS, GLA_DK, GLA_DV), f32),
                        pltpu.VMEM((SSD_GROUPS, SSD_HP, SSD_N), f32), pltpu.VMEM((SUB, SSD_CH), f32)],
        compiler_params=_cp(2),
    )(x, g, w, *lru_params, *gla_params, *ssd_params)


def _merge_kernel(x_ref, g_ref, wg_ref, yl_ref, yg_ref, ys_ref, wl_ref, wgl_ref, ws_ref, wm_ref, o_ref):
    x = x_ref[...]
    gates = jax.nn.sigmoid(_dot(_rms(x, g_ref[...]), wg_ref[...]))
    merged = gates[:, 0:D] * _dot(yl_ref[...], wl_ref[...])
    merged = merged + gates[:, D:2 * D] * _dot(yg_ref[...], wgl_ref[...])
    merged = merged + gates[:, 2 * D:] * _dot(ys_ref[...], ws_ref[...])
    o_ref[...] = x + _dot(merged, wm_ref[...])


def _merge(name, x, g, wgate, yl, yg, ys, wl, wgl, ws, wm, layer, tm):
    m = x.shape[0]
    row = pl.BlockSpec((tm, D), lambda i: (i, 0))
    wsq = _layer_spec((D, D), layer)
    return pl.pallas_call(
        _merge_kernel,
        name=name,
        out_shape=jax.ShapeDtypeStruct((m, D), f32),
        grid=(m // tm,),
        in_specs=[row, _layer_spec((1, D), layer), _layer_spec((D, W_GATE), layer), row, row, row,
                  wsq, wsq, wsq, wsq],
        out_specs=row,
        compiler_params=_cp(1),
    )(x, g, wgate, yl, yg, ys, wl, wgl, ws, wm)


def _attend(q, get_k, get_v, nb, rows):
    outs = []
    for h in range(XA_HEADS):
        hs = slice(h * XA_HD, (h + 1) * XA_HD)
        s = _bdot(q[:, hs], get_k(h), nb, rows, "nt") * XA_HD ** -0.5
        s = s - jnp.max(s, axis=-1, keepdims=True)
        e = jnp.exp(s)
        p = e / jnp.sum(e, axis=-1, keepdims=True)
        outs.append(_bdot(p, get_v(h), nb, rows, "nn"))
    return jnp.concatenate(outs, axis=1)


def _xattn_prompt_kernel(x_ref, g_ref, wq_ref, k_ref, v_ref, wo_ref, o_ref):
    x = x_ref[...]
    q = _dot(_rms(x, g_ref[...]), wq_ref[...])
    o = _attend(q, lambda h: k_ref[:, :, h * XA_HD:(h + 1) * XA_HD],
                lambda h: v_ref[:, :, h * XA_HD:(h + 1) * XA_HD], 1, x.shape[0])
    o_ref[...] = x + _dot(o, wo_ref[...])


def _xattn_sample_kernel(x_ref, g_ref, wq_ref, k_ref, v_ref, wo_ref, o_ref, *, nb, rows):
    x = x_ref[...]
    q = _dot(_rms(x, g_ref[...]), wq_ref[...]).reshape(nb, rows, D)
    qh = jnp.concatenate([q[:, :, h * XA_HD:(h + 1) * XA_HD] for h in range(XA_HEADS)], axis=1)
    k2 = k_ref[...].reshape(nb, MEM * XA_HEADS, XA_HD).astype(bf16)
    v2 = v_ref[...].reshape(nb, MEM * XA_HEADS, XA_HD).astype(bf16)
    s = jnp.einsum("bik,bjk->bij", qh.astype(bf16), k2, preferred_element_type=f32) * XA_HD ** -0.5
    shp = (XA_HEADS * rows, MEM * XA_HEADS)
    q_head = lax.broadcasted_iota(jnp.int32, shp, 0) // rows
    m_head = lax.broadcasted_iota(jnp.int32, shp, 1) & (XA_HEADS - 1)
    s = jnp.where((q_head == m_head)[None], s, -jnp.inf)
    s = s - jnp.max(s, axis=-1, keepdims=True)
    e = jnp.exp(s)
    p = e / jnp.sum(e, axis=-1, keepdims=True)
    o2 = jnp.einsum("bij,bjk->bik", p.astype(bf16), v2, preferred_element_type=f32)
    o = jnp.concatenate([o2[:, h * rows:(h + 1) * rows, :] for h in range(XA_HEADS)], axis=2)
    o_ref[...] = x + _dot(o.reshape(nb * rows, D), wo_ref[...])


def _xattn_prompt(x, g, wq, kv, wo, layer, nseq, seqlen, tq):
    nt = seqlen // tq
    row = pl.BlockSpec((tq, D), lambda b, t: (b * nt + t, 0))
    return pl.pallas_call(
        _xattn_prompt_kernel,
        name="xattn_prompt",
        out_shape=jax.ShapeDtypeStruct(x.shape, f32),
        grid=(nseq, nt),
        in_specs=[row, _layer_spec((1, D), layer), _layer_spec((D, D), layer),
                  pl.BlockSpec((1, MEM, D), lambda b, t: (b, 0, 0)),
                  pl.BlockSpec((1, MEM, D), lambda b, t: (b, 0, 1)), _layer_spec((D, D), layer)],
        out_specs=row,
        compiler_params=_cp(2),
    )(x, g, wq, kv, kv, wo)


def _xattn_sample(x, g, wq, k_all, v_all, wo, layer, nb, rows):
    nseq = k_all.shape[1]
    row = pl.BlockSpec((nb * rows, D), lambda i: (i, 0))
    kvspec = pl.BlockSpec((None, nb, MEM, XA_HEADS, XA_HD), lambda i: (layer, i, 0, 0, 0))
    return pl.pallas_call(
        functools.partial(_xattn_sample_kernel, nb=nb, rows=rows),
        name="xattn_sample",
        out_shape=jax.ShapeDtypeStruct(x.shape, f32),
        grid=(nseq // nb,),
        in_specs=[row, _layer_spec((1, D), layer), _layer_spec((D, D), layer), kvspec, kvspec,
                  _layer_spec((D, D), layer)],
        out_specs=row,
        compiler_params=_cp(1),
    )(x, g, wq, k_all, v_all, wo)


FF_SPLIT = 2


def _ffn_kernel(x_ref, g_ref, wg_ref, wu_ref, wd_ref, o_ref):
    x = x_ref[...]
    h = _rms(x, g_ref[...]).astype(bf16)
    fc = D_FF // FF_SPLIT
    acc = x
    for j in range(FF_SPLIT):
        cs = slice(j * fc, (j + 1) * fc)
        gate = jnp.dot(h, wg_ref[:, cs], preferred_element_type=f32)
        up = jnp.dot(h, wu_ref[:, cs], preferred_element_type=f32)
        acc = acc + _dot(jax.nn.silu(gate) * up, wd_ref[cs, :])
    o_ref[...] = acc


def _ffn(name, x, g, wg, wu, wd, layer, tm):
    m = x.shape[0]
    row = pl.BlockSpec((tm, D), lambda i: (i, 0))
    return pl.pallas_call(
        _ffn_kernel,
        name=name,
        out_shape=jax.ShapeDtypeStruct((m, D), f32),
        grid=(m // tm,),
        in_specs=[row, _layer_spec((1, D), layer), _layer_spec((D, D_FF), layer), _layer_spec((D, D_FF), layer),
                  _layer_spec((D_FF, D), layer)],
        out_specs=row,
        compiler_params=_cp(1),
    )(x, g, wg, wu, wd)


def _prep_w_in_kernel(w_ref, mix_ref, gate_ref, *, o1, o2):
    zpad = jnp.zeros((w_ref.shape[0], LANE - GLA_RANK), bf16)
    lo = O_GLA + W_GLA - LANE
    mix_ref[:, 0:lo] = w_ref[:, 0:lo]
    mix_ref[:, lo:O_SSD] = jnp.concatenate([w_ref[:, lo:o1], zpad], axis=1)
    mix_ref[:, O_SSD:W_MIX] = jnp.concatenate([w_ref[:, o1:o2], zpad], axis=1)
    gate_ref[...] = w_ref[:, o2:]


def _prep_w_in(w_in, o1, o2):
    depth, k, n = w_in.shape
    assert o1 - (O_GLA + W_GLA - LANE) == GLA_RANK and o2 - o1 == W_SSD - (LANE - SSD_HEADS) and n - o2 == W_GATE
    tr = 256
    idx = lambda l, i: (l, i, 0)
    return pl.pallas_call(
        functools.partial(_prep_w_in_kernel, o1=o1, o2=o2),
        name="prep_w_in",
        grid=(depth, k // tr),
        in_specs=[pl.BlockSpec((None, tr, n), idx)],
        out_specs=(pl.BlockSpec((None, tr, W_MIX), idx), pl.BlockSpec((None, tr, W_GATE), idx)),
        out_shape=(jax.ShapeDtypeStruct((depth, k, W_MIX), bf16), jax.ShapeDtypeStruct((depth, k, W_GATE), bf16)),
        compiler_params=_cp(2),
    )(w_in)


def _block_diag_tiles(w):
    depth, nblk, bs, _ = w.shape
    per = LRU_TILE // bs
    w4 = w.reshape(depth, nblk // per, per, bs, bs)
    eye = jnp.eye(per, dtype=w.dtype)
    t = jnp.einsum("ltpij,pq->ltpiqj", w4, eye)
    return t.reshape(depth, nblk // per, LRU_TILE, LRU_TILE).astype(bf16)


def _rows(a):
    return a[:, None, :]


def _pad_last(a, width):
    return jnp.pad(a, [(0, 0)] * (a.ndim - 1) + [(0, width - a.shape[-1])])


def kernel(x_prompt, x_sample, mem_prompt, state_lru_h, state_lru_conv, state_gla_S, state_ssd_h, state_ssd_conv, cache_mem_k, cache_mem_v, norm_mix, w_in, lru_conv_w, lru_conv_b, lru_wa, lru_ba, lru_wx, lru_bx, lru_lambda, w_lru_out, gla_w_alpha, gla_b_alpha, gla_norm, w_gla_out, ssd_conv_w, ssd_conv_b, ssd_dt_bias, ssd_a_log, ssd_d, ssd_norm, w_ssd_out, w_mix_out, norm_xattn, norm_mem, w_xq, w_xk, w_xv, w_xo, norm_ffn, w_ffn_gate, w_ffn_up, w_ffn_down, norm_final):
    bp, lp, _ = x_prompt.shape
    bs, ls, _ = x_sample.shape
    assert ls == SUB, "sample kernels treat each 8-row group as one sequence"
    hk = GLA_HEADS * GLA_DK
    xp = x_prompt.reshape(bp * lp, D)
    xs = x_sample.reshape(bs * ls, D)
    mem = mem_prompt.reshape(bp * MEM, D)

    o1 = 2 * D + 2 * hk + 2 * D + GLA_RANK
    o2 = o1 + D + SSD_CH + SSD_HEADS
    w_mix_in, w_gate = _prep_w_in(w_in.astype(bf16), o1, o2)
    g_mix, g_xa, g_mem, g_ffn = _rows(norm_mix), _rows(norm_xattn), _rows(norm_mem), _rows(norm_ffn)
    lru_params = (lru_conv_w, _rows(lru_conv_b), _rows(lru_lambda), _block_diag_tiles(lru_wa), _rows(lru_ba),
                  _block_diag_tiles(lru_wx), _rows(lru_bx))
    gla_params = (jnp.pad(gla_w_alpha, ((0, 0), (0, LANE - GLA_RANK), (0, 0))).astype(bf16),
                  _rows(gla_b_alpha), _rows(gla_norm))
    expand = (jnp.arange(LANE)[:, None] == (jnp.arange(D)[None, :] // SSD_P)).astype(bf16)
    ssd_params = (ssd_conv_w, _rows(ssd_conv_b), _rows(_pad_last(ssd_dt_bias, LANE)),
                  _rows(_pad_last(ssd_a_log, LANE)), _rows(jnp.repeat(ssd_d, SSD_P, axis=1)),
                  _rows(ssd_norm), expand)
    w_lo, w_go, w_so, w_mo = (w.astype(bf16) for w in (w_lru_out, w_gla_out, w_ssd_out, w_mix_out))
    wq, wo = w_xq.astype(bf16), w_xo.astype(bf16)
    wkv = jnp.concatenate([w_xk, w_xv], axis=2).astype(bf16)
    wfg, wfu, wfd = w_ffn_gate.astype(bf16), w_ffn_up.astype(bf16), w_ffn_down.astype(bf16)
    pad5 = ((0, 0), (0, 0), (SUB - (CONV_W - 1), 0), (0, 0))
    bp_lru = jnp.pad(state_lru_conv, pad5).reshape(DEPTH, bs * ls, D)
    bp_ssd = jnp.pad(state_ssd_conv, pad5).reshape(DEPTH, bs * ls, SSD_CH)
    h0_lru = state_lru_h[:, :, None, :]

    p_out = [[] for _ in range(7)]
    s_out = [[] for _ in range(4)]
    s_gla_new = None
    for l in range(DEPTH):
        kv, k4, v4 = _mem_kv(mem, g_mem, wkv, l, T_ROW)
        yl, yg, ys, p_h, p_tail, p_s, p_sh, p_stail = _mix_prompt(
            xp, g_mix, w_mix_in, lru_params, gla_params, ssd_params, l, bp, lp, T_MIX)
        xp = _merge("merge_prompt", xp, g_mix, w_gate, yl, yg, ys, w_lo, w_go, w_so, w_mo, l, T_MERGE)
        xp = _xattn_prompt(xp, g_xa, wq, kv.reshape(bp, MEM, 2 * D), wo, l, bp, lp, T_XATTN)
        xp = _ffn("ffn_prompt", xp, g_ffn, wfg, wfu, wfd, l, T_FFN)
        p_vals = (p_h[:, 0], p_tail[:, SUB - 3:], p_s, p_sh, p_stail[:, SUB - 3:],
                  k4.reshape(bp, MEM, XA_HEADS, XA_HD), v4.reshape(bp, MEM, XA_HEADS, XA_HD))
        for lst, val in zip(p_out, p_vals):
            lst.append(val)

        u_lru, u_gla, u_ssd = _in_proj_sample(xs, g_mix, w_mix_in, l, T_SAMPLE)
        yl, h_last = _lru_sample(u_lru, h0_lru, bp_lru[l], lru_params, l, T_SAMPLE)
        yg, s_gla_new = _gla_sample(u_gla, state_gla_S, gla_params, l, NB_STATE, ls, s_gla_new)
        ys, s_sh = _ssd_sample(u_ssd, bp_ssd[l], state_ssd_h, ssd_params, l, NB_STATE, ls)
        xs = _merge("merge_sample", xs, g_mix, w_gate, yl, yg, ys, w_lo, w_go, w_so, w_mo, l, T_SAMPLE)
        xs = _xattn_sample(xs, g_xa, wq, cache_mem_k, cache_mem_v, wo, l, NB_STATE, ls)
        xs = _ffn("ffn_sample", xs, g_ffn, wfg, wfu, wfd, l, T_SAMPLE)
        s_vals = (h_last[:, 0], u_lru.reshape(bs, ls, W_LRU)[:, ls - 3:, :D], s_sh,
                  u_ssd.reshape(bs, ls, W_SSD)[:, ls - 3:, D:D + SSD_CH])
        for lst, val in zip(s_out, s_vals):
            lst.append(val)

    y_prompt = _final_norm("final_norm_prompt", xp, norm_final.reshape(1, D), T_ROW).reshape(bp, lp, D)
    y_sample = _final_norm("final_norm_sample", xs, norm_final.reshape(1, D), T_SAMPLE).reshape(bs, ls, D)
    p_stack = [jnp.stack(t, axis=0) for t in p_out]
    s_lru_h, s_lru_conv, s_ssd_h, s_ssd_conv = [jnp.stack(t, axis=0) for t in s_out]
    return (y_prompt, y_sample, *p_stack, s_lru_h, s_lru_conv, s_gla_new, s_ssd_h, s_ssd_conv)
```
